```python
import jax, jax.numpy as jnp
from jax import lax
import numpy as np

D_MODEL = 1024
BATCH = 8
SEQ = 8192
DEPTH = 4

PLE_DIM = 256
D_FF = 2816
Q_BLOCK = 128
HGRN_CHUNK = 32
EPS = 1e-6
N_BRANCH = 4
FOX_HEADS = 4
FOX_DIM = 64
FOX_W = FOX_HEADS * FOX_DIM
MLA_HEADS = 4
MLA_NOPE = 64
MLA_ROPE = 32
MLA_V = 64
MLA_Q_RANK = 256
MLA_KV_RANK = 128
MLA_W = MLA_HEADS * MLA_V
ROPE_BASE = 10000.0
SB_HEADS = 4
SB_DIM = 64
SB_W = SB_HEADS * SB_DIM
HG_HEADS = 4
HG_DK = 64
HG_DV = 64
HG_W = HG_HEADS * HG_DK
BRANCH_W = 256
IN_SPLITS = (FOX_W, FOX_W, FOX_W, FOX_HEADS,
             MLA_Q_RANK, MLA_KV_RANK, MLA_ROPE,
             SB_W, SB_W, SB_W,
             HG_W, HG_W, HG_W, HG_W,
             N_BRANCH * D_MODEL)
IN_COLS = sum(IN_SPLITS)

kernel_name = "hybrid_fox_mla_stickbreak_hgrn2_macaron"

F32 = jnp.float32


def rmsnorm(x, g):
    xf = x.astype(F32)
    y = xf * lax.rsqrt(jnp.mean(xf * xf, axis=-1, keepdims=True) + EPS)
    return (y * g.astype(F32)).astype(x.dtype)


def heads(t, h):
    b, s, _ = t.shape
    return t.reshape(b, s, h, -1).transpose(0, 2, 1, 3)


def merge_heads(t):
    b, h, s, d = t.shape
    return t.transpose(0, 2, 1, 3).reshape(b, s, h * d)


def block_cumsum(x, blk):
    *lead, n = x.shape
    nb = n // blk
    xb = x.reshape(*lead, nb, blk)
    within = jnp.einsum('...ks,ts->...kt', xb, jnp.tril(jnp.ones((blk, blk), x.dtype)))
    prev = jnp.einsum('...k,jk->...j', within[..., -1], jnp.tril(jnp.ones((nb, nb), x.dtype), -1))
    return (within + prev[..., None]).reshape(*lead, n)


def suffix_sum_exclusive(x, blk):
    *lead, n = x.shape
    nb = n // blk
    xb = x.reshape(*lead, nb, blk)
    within = jnp.einsum('...ks,ts->...kt', xb, jnp.triu(jnp.ones((blk, blk), x.dtype), 1))
    later = jnp.einsum('...k,jk->...j', jnp.sum(xb, axis=-1), jnp.triu(jnp.ones((nb, nb), x.dtype), 1))
    return (within + later[..., None]).reshape(*lead, n)


def rotary(x, pos):
    half = x.shape[-1] // 2
    inv = ROPE_BASE ** (-jnp.arange(half, dtype=F32) / half)
    ang = pos.astype(F32)[:, None] * inv[None, :]
    cos, sin = jnp.cos(ang).astype(x.dtype), jnp.sin(ang).astype(x.dtype)
    x1, x2 = x[..., :half], x[..., half:]
    return jnp.concatenate([x1 * cos - x2 * sin, x2 * cos + x1 * sin], axis=-1)


def swiglu(h, wi, wo):
    g, u = jnp.split(h @ wi, 2, axis=-1)
    return (jax.nn.silu(g) * u) @ wo


def causal_softmax_attention(q, k, v, scale, decay=None):
    s_len = q.shape[2]
    q = q * scale
    outs = []
    for i in range(s_len // Q_BLOCK):
        lo, hi = i * Q_BLOCK, (i + 1) * Q_BLOCK
        s = jnp.einsum('bhqd,bhkd->bhqk', q[:, :, lo:hi], k[:, :, :hi], preferred_element_type=F32)
        if decay is not None:
            s = s + decay[:, :, lo:hi, None] - decay[:, :, None, :hi]
        allowed = jnp.arange(hi)[None, :] <= (lo + jnp.arange(Q_BLOCK))[:, None]
        s = jnp.where(allowed, s, -jnp.inf)
        e = jnp.exp(s - jnp.max(s, axis=-1, keepdims=True))
        o = jnp.einsum('bhqk,bhkd->bhqd', e.astype(v.dtype), v[:, :, :hi], preferred_element_type=F32)
        outs.append((o / jnp.sum(e, axis=-1, keepdims=True)).astype(v.dtype))
    return jnp.concatenate(outs, axis=2)


def stick_breaking_attention(q, k, v, scale):
    s_len = q.shape[2]
    q = q * scale
    outs = []
    for i in range(s_len // Q_BLOCK):
        lo, hi = i * Q_BLOCK, (i + 1) * Q_BLOCK
        z = jnp.einsum('bhqd,bhkd->bhqk', q[:, :, lo:hi], k[:, :, :hi], preferred_element_type=F32)
        strict = jnp.arange(hi)[None, :] < (lo + jnp.arange(Q_BLOCK))[:, None]
        log_beta = jax.nn.log_sigmoid(z)
        log_rest = jnp.where(strict, log_beta - z, 0.0)
        later = suffix_sum_exclusive(log_rest, Q_BLOCK)
        a = jnp.where(strict, jnp.exp(log_beta + later), 0.0)
        outs.append(jnp.einsum('bhqk,bhkd->bhqd', a.astype(v.dtype), v[:, :, :hi]))
    return jnp.concatenate(outs, axis=2)


def hgrn2_chunkwise(q, k, v, log_f):
    b, h, s_len, dk = q.shape
    dv = v.shape[-1]
    c = HGRN_CHUNK
    n = s_len // c

    def chunks(t):
        return jnp.moveaxis(t.astype(F32).reshape(b, h, n, c, t.shape[-1]), 2, 0)

    tril = jnp.tril(jnp.ones((c, c), F32))
    cum = jnp.einsum('nbhsd,ts->nbhtd', chunks(log_f), tril)
    causal = jnp.tril(jnp.ones((c, c), dtype=bool))[:, :, None]

    def step(state, args):
        qc, kc, vc, bc = args
        diff = bc[:, :, :, None, :] - bc[:, :, None, :, :]
        dec = jnp.exp(jnp.where(causal, diff, -jnp.inf))
        scores = jnp.einsum('bhtsd,bhsd->bhts', qc[:, :, :, None, :] * dec, kc)
        o = jnp.einsum('bhts,bhsv->bhtv', scores, vc)
        o = o + jnp.einsum('bhtd,bhdv->bhtv', qc * jnp.exp(bc), state)
        b_last = bc[:, :, -1, :]
        k_dec = kc * jnp.exp(b_last[:, :, None, :] - bc)
        new_state = state * jnp.exp(b_last)[..., None] + jnp.einsum('bhsd,bhsv->bhdv', k_dec, vc)
        return new_state, o

    state0 = jnp.zeros((b, h, dk, dv), F32)
    _, o = lax.scan(step, state0, (chunks(q), chunks(k), chunks(v), cum))
    return jnp.moveaxis(o, 0, 2).reshape(b, h, s_len, dv)


def token_mixing(h, lb, w_in, b_f, q_norm, w_uq, kv_norm, w_ukv, out_norm, w_branch, w_out, pos):
    bsz, s_len, _ = h.shape
    offsets = [int(o) for o in np.cumsum(IN_SPLITS)[:-1]]
    (fq, fk, fv, ff, mcq, mckv, mkr, sq, sk, sv, hq, hf, hi, hg, gates) = jnp.split(h @ w_in, offsets, axis=-1)

    log_f = jax.nn.log_sigmoid((ff + b_f).astype(F32))
    decay = block_cumsum(log_f.transpose(0, 2, 1), Q_BLOCK)
    y_a = causal_softmax_attention(heads(fq, FOX_HEADS), heads(fk, FOX_HEADS), heads(fv, FOX_HEADS),
                                   FOX_DIM ** -0.5, decay)

    qm = heads(rmsnorm(mcq, q_norm) @ w_uq, MLA_HEADS)
    kvm = heads(rmsnorm(mckv, kv_norm) @ w_ukv, MLA_HEADS)
    q_b = jnp.concatenate([qm[..., :MLA_NOPE], rotary(qm[..., MLA_NOPE:], pos)], axis=-1)
    k_rope = jnp.broadcast_to(rotary(mkr[:, None], pos), (bsz, MLA_HEADS, s_len, MLA_ROPE))
    k_b = jnp.concatenate([kvm[..., :MLA_NOPE], k_rope], axis=-1)
    y_b = causal_softmax_attention(q_b, k_b, kvm[..., MLA_NOPE:], (MLA_NOPE + MLA_ROPE) ** -0.5)

    y_c = stick_breaking_attention(heads(sq, SB_HEADS), heads(sk, SB_HEADS), heads(sv, SB_HEADS), SB_DIM ** -0.5)

    f = lb + (1.0 - lb) * jax.nn.sigmoid(hf.astype(F32))
    o_d = hgrn2_chunkwise(heads(hq, HG_HEADS), heads(1.0 - f, HG_HEADS), heads(hi, HG_HEADS),
                          heads(jnp.log(f), HG_HEADS)).astype(h.dtype)
    o_d = rmsnorm(o_d, out_norm.reshape(HG_HEADS, 1, HG_DV))
    y_d = merge_heads(o_d) * jax.nn.silu(hg)

    ys = jnp.stack([merge_heads(y_a), merge_heads(y_b), merge_heads(y_c), y_d], axis=2)
    proj = jnp.einsum('bsnc,ncd->bsnd', ys, w_branch)
    g = jax.nn.sigmoid(gates).reshape(bsz, s_len, N_BRANCH, D_MODEL)
    return jnp.sum(g * proj, axis=2) @ w_out


def setup_inputs(seed: int = 0) -> dict:
    key = jax.random.key(seed)
    ks = jax.random.split(key, 24)
    nrm = lambda k, shape, s: jax.random.normal(k, shape, F32) * s
    gain = lambda k, shape: 1.0 + 0.05 * jax.random.normal(k, shape, F32)
    return {
        "x": nrm(ks[0], (BATCH, SEQ, D_MODEL), 1.0),
        "p": nrm(ks[1], (DEPTH, BATCH, SEQ, PLE_DIM), 1.0),
        "w_in": nrm(ks[2], (DEPTH, D_MODEL, IN_COLS), D_MODEL ** -0.5),
        "b_fox_f": 3.0 + nrm(ks[3], (DEPTH, FOX_HEADS), 0.5),
        "mla_q_norm": gain(ks[4], (DEPTH, MLA_Q_RANK)),
        "w_mla_uq": nrm(ks[5], (DEPTH, MLA_Q_RANK, MLA_HEADS * (MLA_NOPE + MLA_ROPE)), MLA_Q_RANK ** -0.5),
        "mla_kv_norm": gain(ks[6], (DEPTH, MLA_KV_RANK)),
        "w_mla_ukv": nrm(ks[7], (DEPTH, MLA_KV_RANK, MLA_HEADS * (MLA_NOPE + MLA_V)), MLA_KV_RANK ** -0.5),
        "hgrn_lb_logits": nrm(ks[8], (DEPTH, HG_W), 0.1),
        "hgrn_out_norm": gain(ks[9], (DEPTH, HG_W)),
        "w_branch": nrm(ks[10], (DEPTH, N_BRANCH, BRANCH_W, D_MODEL), BRANCH_W ** -0.5),
        "w_out": nrm(ks[11], (DEPTH, D_MODEL, D_MODEL), 0.5 * D_MODEL ** -0.5),
        "ffn_a_wi": nrm(ks[12], (DEPTH, D_MODEL, 2 * D_FF), D_MODEL ** -0.5),
        "ffn_a_wo": nrm(ks[13], (DEPTH, D_FF, D_MODEL), D_FF ** -0.5),
        "ffn_b_wi": nrm(ks[14], (DEPTH, D_MODEL, 2 * D_FF), D_MODEL ** -0.5),
        "ffn_b_wo": nrm(ks[15], (DEPTH, D_FF, D_MODEL), D_FF ** -0.5),
        "w_ple_in": nrm(ks[16], (DEPTH, PLE_DIM, D_MODEL), 0.5 * PLE_DIM ** -0.5),
        "w_ple_gate": nrm(ks[17], (DEPTH, D_MODEL, D_MODEL), D_MODEL ** -0.5),
        "norms": gain(ks[18], (DEPTH, 4, D_MODEL)),
        "final_norm": gain(ks[19], (D_MODEL,)),
    }


def reference(x, p, w_in, b_fox_f, mla_q_norm, w_mla_uq, mla_kv_norm, w_mla_ukv, hgrn_lb_logits,
              hgrn_out_norm, w_branch, w_out, ffn_a_wi, ffn_a_wo, ffn_b_wi, ffn_b_wo,
              w_ple_in, w_ple_gate, norms, final_norm):
    pos = jnp.arange(x.shape[1])
    lb_soft = jax.nn.softmax(hgrn_lb_logits.astype(F32), axis=0)
    lb_cum = jnp.cumsum(lb_soft, axis=0)
    lower_bounds = lb_cum - lb_cum[0]
    for i in range(DEPTH):
        x = x + 0.5 * swiglu(rmsnorm(x, norms[i, 0]), ffn_a_wi[i], ffn_a_wo[i])
        x = x + token_mixing(rmsnorm(x, norms[i, 1]), lower_bounds[i], w_in[i], b_fox_f[i],
                             mla_q_norm[i], w_mla_uq[i], mla_kv_norm[i], w_mla_ukv[i],
                             hgrn_out_norm[i], w_branch[i], w_out[i], pos)
        x = x + 0.5 * swiglu(rmsnorm(x, norms[i, 2]), ffn_b_wi[i], ffn_b_wo[i])
        x = x + jax.nn.sigmoid(rmsnorm(x, norms[i, 3]) @ w_ple_gate[i]) * (p[i] @ w_ple_in[i])
    return rmsnorm(x, final_norm)
```

```python
import functools

import jax
import jax.numpy as jnp
import numpy as np
from jax import lax
from jax.experimental import pallas as pl
from jax.experimental.pallas import tpu as pltpu

F32 = jnp.float32
BF16 = jnp.bfloat16

D_MODEL = 1024
PLE_DIM = 256
D_FF = 2816
EPS = 1e-6
N_BRANCH = 4
N_HEADS = 4
HEAD_DIM = 64
BRANCH_W = 256
MLA_NOPE = 64
MLA_ROPE = 32
MLA_Q_RANK = 256
MLA_KV_RANK = 128
ROPE_BASE = 10000.0
HGRN_CHUNK = 32
LANES = 128

IN_SPLITS = (256, 256, 256, 4, 256, 128, 32, 256, 256, 256, 256, 256, 256, 256, 4096)
IN_OFFS = tuple(int(o) for o in np.cumsum((0,) + IN_SPLITS))

PRE_COLS = dict(
    fq=(0, 256), fk=(256, 512), fv=(512, 768), ff=(768, 896),
    mcq=(896, 1152), mckv=(1152, 1280), mkr=(1280, 1408), mkr_sw=(1408, 1536),
    sq=(1536, 1792), sk=(1792, 2048), sv=(2048, 2304),
    hq=(2304, 2560), hf=(2560, 2816), hi=(2816, 3072), hg=(3072, 3328),
)
PRE_W = 3328

FF_CHUNK = 256
TOKEN_TILE = 512
ATTN_TILE = 256
HGRN_TILE = 256
VMEM_LIMIT = 56 * 1024 * 1024


def _dot(a, b):
    return jnp.dot(a, b, preferred_element_type=F32)


def _dot_nt(a, b):
    return lax.dot_general(a, b, (((1,), (1,)), ((), ())), preferred_element_type=F32)


def _rms(x, g):
    return x * lax.rsqrt(jnp.mean(x * x, axis=-1, keepdims=True) + EPS) * g


def _log_sigmoid(x):
    return jnp.minimum(x, 0.0) - jnp.log1p(jnp.exp(-jnp.abs(x)))


def _split3(x):
    hi = x.astype(BF16)
    r1 = x - hi.astype(F32)
    mid = r1.astype(BF16)
    lo = (r1 - mid.astype(F32)).astype(BF16)
    return hi, mid, lo


def _params(*sem):
    return pltpu.CompilerParams(dimension_semantics=sem, vmem_limit_bytes=VMEM_LIMIT)


def _const_spec(shape):
    nd = len(shape)
    return pl.BlockSpec(shape, lambda *_: (0,) * nd, pipeline_mode=pl.Buffered(1))


def _ffn_kernel(x_ref, g_ref, wg_ref, wu_ref, wo_ref, o_ref, acc_ref):
    x = x_ref[...]
    h = _rms(x, g_ref[...]).astype(BF16)
    for c in range(D_FF // FF_CHUNK):
        sl = slice(c * FF_CHUNK, (c + 1) * FF_CHUNK)
        g = _dot(h, wg_ref[:, sl])
        u = _dot(h, wu_ref[:, sl])
        a = (g * jax.nn.sigmoid(g) * u).astype(BF16)
        part = _dot(a, wo_ref[sl, :])
        if c == 0:
            acc_ref[...] = part
        else:
            acc_ref[...] += part
    o_ref[...] = x + 0.5 * acc_ref[...]


def _ffn(x2, g, wi, wo):
    n = x2.shape[0]
    tm = min(TOKEN_TILE, n)
    wg = wi[:, :D_FF].astype(BF16)
    wu = wi[:, D_FF:].astype(BF16)
    return pl.pallas_call(
        _ffn_kernel,
        out_shape=jax.ShapeDtypeStruct((n, D_MODEL), F32),
        grid=(n // tm,),
        in_specs=[
            pl.BlockSpec((tm, D_MODEL), lambda i: (i, 0)),
            _const_spec((1, D_MODEL)),
            _const_spec((D_MODEL, D_FF)),
            _const_spec((D_MODEL, D_FF)),
            _const_spec((D_FF, D_MODEL)),
        ],
        out_specs=pl.BlockSpec((tm, D_MODEL), lambda i: (i, 0)),
        scratch_shapes=[pltpu.VMEM((tm, D_MODEL), F32)],
        compiler_params=_params("parallel"),
        name="ffn",
    )(x2, g.reshape(1, D_MODEL), wg, wu, wo.astype(BF16))


def _ple_kernel(x_ref, p_ref, g_ref, wpg_ref, wpe_ref, *rest, final):
    o_ref = rest[-1]
    x = x_ref[...]
    h = _rms(x, g_ref[...]).astype(BF16)
    gate = jax.nn.sigmoid(_dot(h, wpg_ref[...]))
    y = x + gate * _dot(p_ref[...].astype(BF16), wpe_ref[...])
    if final:
        y = _rms(y, rest[0][...])
    o_ref[...] = y


def _ple(x2, p2, g, wpg, wpe, final_g=None):
    n = x2.shape[0]
    tm = min(TOKEN_TILE, n)
    final = final_g is not None
    in_specs = [
        pl.BlockSpec((tm, D_MODEL), lambda i: (i, 0)),
        pl.BlockSpec((tm, PLE_DIM), lambda i: (i, 0)),
        _const_spec((1, D_MODEL)),
        _const_spec((D_MODEL, D_MODEL)),
        _const_spec((PLE_DIM, D_MODEL)),
    ]
    args = [x2, p2, g.reshape(1, D_MODEL), wpg.astype(BF16), wpe.astype(BF16)]
    if final:
        in_specs.append(_const_spec((1, D_MODEL)))
        args.append(final_g.reshape(1, D_MODEL))
    return pl.pallas_call(
        functools.partial(_ple_kernel, final=final),
        out_shape=jax.ShapeDtypeStruct((n, D_MODEL), F32),
        grid=(n // tm,),
        in_specs=in_specs,
        out_specs=pl.BlockSpec((tm, D_MODEL), lambda i: (i, 0)),
        compiler_params=_params("parallel"),
        name="ple_final" if final else "ple",
    )(*args)


def _pre_kernel(x_ref, g_ref, w_ref, bf_ref, tril_ref, selq_ref, selk_ref, rowq_ref, rowk_ref,
                qn_ref, kvn_ref, wuqa_ref, wuqb_ref, wukvk_ref, wukvv_ref, rope_ref, lb_ref,
                fq_ref, fk_ref, fv_ref, mq_ref, mk_ref, mv_ref, sq_ref, sk_ref, sv_ref,
                hq_ref, hk_ref, hlf_ref, hv_ref, hg_ref, carry_ref):
    @pl.when(pl.program_id(1) == 0)
    def _():
        carry_ref[...] = jnp.zeros_like(carry_ref)

    h = _rms(x_ref[0], g_ref[...]).astype(BF16)

    def proj(name):
        lo, hi = PRE_COLS[name]
        return _dot(h, w_ref[:, lo:hi])

    log_f = _log_sigmoid(proj("ff") + bf_ref[...])
    tril = tril_ref[...]
    c = carry_ref[...]
    for piece in _split3(log_f):
        c = c + _dot(tril, piece)
    carry_ref[...] = c[c.shape[0] - 1:, :]
    c_parts = jnp.concatenate(_split3(c), axis=-1)
    cq = (_dot(c_parts, selq_ref[...]) + rowq_ref[...]).astype(BF16)
    ck = (_dot(c_parts, selk_ref[...]) + rowk_ref[...]).astype(BF16)
    fq = proj("fq").astype(BF16)
    fk = proj("fk").astype(BF16)
    for p in range(2):
        pair = slice(p * LANES, (p + 1) * LANES)
        fq_ref[0, :, 2 * p * LANES:(2 * p + 1) * LANES] = fq[:, pair]
        fq_ref[0, :, (2 * p + 1) * LANES:(2 * p + 2) * LANES] = cq[:, pair]
        fk_ref[0, :, 2 * p * LANES:(2 * p + 1) * LANES] = fk[:, pair]
        fk_ref[0, :, (2 * p + 1) * LANES:(2 * p + 2) * LANES] = ck[:, pair]
    fv_ref[0] = proj("fv").astype(BF16)

    rope = rope_ref[...]
    cos_q, sin_q = rope[:, 0:LANES], rope[:, LANES:2 * LANES]
    cos_k, sin_k = rope[:, 2 * LANES:3 * LANES], rope[:, 3 * LANES:4 * LANES]
    cq_lat = _rms(proj("mcq"), qn_ref[...]).astype(BF16)
    qa = _dot(cq_lat, wuqa_ref[...])
    qb = _dot(cq_lat, wuqb_ref[...])
    ckv_lat = _rms(proj("mckv"), kvn_ref[...]).astype(BF16)
    k_nope = _dot(ckv_lat, wukvk_ref[...])
    k_rope = proj("mkr") * cos_k + proj("mkr_sw") * sin_k
    for hd in range(N_HEADS):
        sl = slice(hd * LANES, (hd + 1) * LANES)
        mq_ref[0, :, sl] = (qa[:, sl] * cos_q + qb[:, sl] * sin_q).astype(BF16)
        mk_ref[0, :, sl] = (k_nope[:, sl] + k_rope).astype(BF16)
    mv_ref[0] = _dot(ckv_lat, wukvv_ref[...]).astype(BF16)

    sq_ref[0] = proj("sq").astype(BF16)
    sk_ref[0] = proj("sk").astype(BF16)
    sv_ref[0] = proj("sv").astype(BF16)

    lb = lb_ref[...]
    f = lb + (1.0 - lb) * jax.nn.sigmoid(proj("hf"))
    hq_ref[0] = proj("hq").astype(BF16)
    hk_ref[0] = 1.0 - f
    hlf_ref[0] = jnp.log(f)
    hv_ref[0] = proj("hi").astype(BF16)
    hg = proj("hg")
    hg_ref[0] = (hg * jax.nn.sigmoid(hg)).astype(BF16)


def _pre_weights(w_in, b_f, w_uq, w_ukv):
    cols = lambda j: w_in[:, IN_OFFS[j]:IN_OFFS[j + 1]]
    zeros = lambda n: jnp.zeros((D_MODEL, n), F32)
    half = MLA_ROPE // 2
    mkr = cols(6)
    mkr_sw = jnp.concatenate([-mkr[:, half:], mkr[:, :half]], axis=1)
    pad_rope = lambda t: jnp.concatenate([zeros(MLA_NOPE), t, zeros(LANES - MLA_NOPE - MLA_ROPE)], axis=1)
    scale = HEAD_DIM ** -0.5
    w = jnp.concatenate([
        cols(0) * scale, cols(1), cols(2),
        jnp.concatenate([cols(3), zeros(LANES - N_HEADS)], axis=1),
        cols(4), cols(5), pad_rope(mkr), pad_rope(mkr_sw),
        cols(7) * scale, cols(8), cols(9),
        cols(10), cols(11), cols(12), cols(13)], axis=1).astype(BF16)
    bf = jnp.concatenate([b_f, jnp.zeros((LANES - N_HEADS,), F32)]).reshape(1, LANES)

    dq = MLA_NOPE + MLA_ROPE
    uq = w_uq.reshape(MLA_Q_RANK, N_HEADS, dq)
    zq = jnp.zeros((MLA_Q_RANK, N_HEADS, LANES - dq), F32)
    uqa = jnp.concatenate([uq, zq], axis=2).reshape(MLA_Q_RANK, N_HEADS * LANES)
    rope_cols = uq[:, :, MLA_NOPE:]
    rope_sw = jnp.concatenate([-rope_cols[:, :, half:], rope_cols[:, :, :half]], axis=2)
    uqb = jnp.concatenate([jnp.zeros((MLA_Q_RANK, N_HEADS, MLA_NOPE), F32), rope_sw, zq],
                          axis=2).reshape(MLA_Q_RANK, N_HEADS * LANES)
    ukv = w_ukv.reshape(MLA_KV_RANK, N_HEADS, MLA_NOPE + HEAD_DIM)
    ukvk = jnp.concatenate([ukv[:, :, :MLA_NOPE], jnp.zeros((MLA_KV_RANK, N_HEADS, LANES - MLA_NOPE), F32)],
                           axis=2).reshape(MLA_KV_RANK, N_HEADS * LANES)
    ukvv = ukv[:, :, MLA_NOPE:].reshape(MLA_KV_RANK, N_HEADS * HEAD_DIM)
    return w, bf, uqa.astype(BF16), uqb.astype(BF16), ukvk.astype(BF16), ukvv.astype(BF16)


def _fox_selectors():
    selq = np.zeros((3 * LANES, 2 * LANES), np.float32)
    selk = np.zeros((3 * LANES, 2 * LANES), np.float32)
    rowq = np.zeros((1, 2 * LANES), np.float32)
    rowk = np.zeros((1, 2 * LANES), np.float32)
    for hd in range(N_HEADS):
        base = (hd // 2) * LANES + (hd % 2) * 8
        for piece in range(3):
            selq[piece * LANES + hd, base + piece] = 1.0
            selk[piece * LANES + hd, base + 3 + piece] = -1.0
            rowq[0, base + 3 + piece] = 1.0
            rowk[0, base + piece] = 1.0
    return (jnp.asarray(selq, BF16), jnp.asarray(selk, BF16), jnp.asarray(rowq), jnp.asarray(rowk))


def _rope_tables(s_len):
    half = MLA_ROPE // 2
    inv = ROPE_BASE ** (-jnp.arange(half, dtype=F32) / half)
    ang = jnp.arange(s_len).astype(F32)[:, None] * inv[None, :]
    cos, sin = jnp.cos(ang), jnp.sin(ang)
    tail = jnp.zeros((s_len, LANES - MLA_NOPE - MLA_ROPE), F32)
    cos_t = jnp.concatenate([jnp.ones((s_len, MLA_NOPE), F32), cos, cos, tail], axis=1)
    sin_t = jnp.concatenate([jnp.zeros((s_len, MLA_NOPE), F32), sin, sin, tail], axis=1)
    scale = (MLA_NOPE + MLA_ROPE) ** -0.5
    return jnp.concatenate([cos_t * scale, sin_t * scale, cos_t, sin_t], axis=1)


def _pre(x3, g, w, bf, uqa, uqb, ukvk, ukvv, qn, kvn, lb, consts):
    b, s_len, _ = x3.shape
    tm = min(TOKEN_TILE, s_len)
    tril, selq, selk, rowq, rowk, rope = consts
    tok = lambda width: pl.BlockSpec((1, tm, width), lambda bi, si: (bi, si, 0))
    bf16_out = lambda width: jax.ShapeDtypeStruct((b, s_len, width), BF16)
    f32_out = lambda width: jax.ShapeDtypeStruct((b, s_len, width), F32)
    in_specs = [
        tok(D_MODEL), _const_spec((1, D_MODEL)), _const_spec((D_MODEL, PRE_W)), _const_spec((1, LANES)),
        _const_spec((tm, tm)), _const_spec(selq.shape), _const_spec(selk.shape),
        _const_spec(rowq.shape), _const_spec(rowk.shape),
        _const_spec((1, MLA_Q_RANK)), _const_spec((1, MLA_KV_RANK)),
        _const_spec(uqa.shape), _const_spec(uqb.shape), _const_spec(ukvk.shape), _const_spec(ukvv.shape),
        pl.BlockSpec((tm, 4 * LANES), lambda bi, si: (si, 0)),
        _const_spec((1, BRANCH_W)),
    ]
    widths = [(512, BF16), (512, BF16), (256, BF16), (512, BF16), (512, BF16), (256, BF16),
              (256, BF16), (256, BF16), (256, BF16),
              (256, BF16), (256, F32), (256, F32), (256, BF16), (256, BF16)]
    return pl.pallas_call(
        _pre_kernel,
        out_shape=[bf16_out(wd) if dt == BF16 else f32_out(wd) for wd, dt in widths],
        grid=(b, s_len // tm),
        in_specs=in_specs,
        out_specs=[tok(wd) for wd, _ in widths],
        scratch_shapes=[pltpu.VMEM((1, LANES), F32)],
        compiler_params=_params("arbitrary", "arbitrary"),
        name="pre_mix",
    )(x3, g.reshape(1, D_MODEL), w, bf, tril, selq, selk, rowq, rowk,
      qn.reshape(1, -1), kvn.reshape(1, -1), uqa, uqb, ukvk, ukvv, rope, lb.reshape(1, -1))


def _lane_mask(shape, ranges):
    lane = lax.broadcasted_iota(jnp.int32, shape, 1)
    m = None
    for lo, hi in ranges:
        r = (lane >= lo) & (lane < hi)
        m = r if m is None else (m | r)
    return m


def _softmax_attn_kernel(q_ref, k_ref, v_ref, o_ref, *, t, lanes_a, lanes_b):
    qi = pl.program_id(2)
    q = q_ref[0]
    zero = jnp.zeros_like(q)
    qs = (jnp.where(_lane_mask(q.shape, lanes_a), q, zero), jnp.where(_lane_mask(q.shape, lanes_b), q, zero))
    row = lax.broadcasted_iota(jnp.int32, (t, t), 0)
    col = lax.broadcasted_iota(jnp.int32, (t, t), 1)
    causal = col <= row

    start = pl.multiple_of(qi * t, t)
    kd = k_ref[0, pl.ds(start, t), :]
    vd = v_ref[0, pl.ds(start, t), :]

    def first(qx):
        s = jnp.where(causal, _dot_nt(qx, kd), -1e30)
        m = jnp.max(s, axis=-1, keepdims=True)
        p = jnp.exp(s - m)
        return m, jnp.sum(p, axis=-1, keepdims=True), _dot(p.astype(BF16), vd)

    def body(j, carry):
        off = pl.multiple_of(j * t, t)
        kb = k_ref[0, pl.ds(off, t), :]
        vb = v_ref[0, pl.ds(off, t), :]
        out = []
        for qx, (m, l, acc) in zip(qs, carry):
            s = _dot_nt(qx, kb)
            m_new = jnp.maximum(m, jnp.max(s, axis=-1, keepdims=True))
            alpha = jnp.exp(m - m_new)
            p = jnp.exp(s - m_new)
            out.append((m_new, alpha * l + jnp.sum(p, axis=-1, keepdims=True),
                        alpha * acc + _dot(p.astype(BF16), vb)))
        return tuple(out)

    (_, la, acca), (_, lb, accb) = lax.fori_loop(0, qi, body, (first(qs[0]), first(qs[1])))
    first_half = _lane_mask(acca.shape, ((0, HEAD_DIM),))
    o_ref[0] = jnp.where(first_half, acca / la, accb / lb).astype(BF16)


def _softmax_attn(q, k, v, kw, lanes_a, lanes_b, name):
    b, s_len, _ = q.shape
    t = min(ATTN_TILE, s_len)
    return pl.pallas_call(
        functools.partial(_softmax_attn_kernel, t=t, lanes_a=lanes_a, lanes_b=lanes_b),
        out_shape=jax.ShapeDtypeStruct((b, s_len, BRANCH_W), BF16),
        grid=(b, 2, s_len // t),
        in_specs=[
            pl.BlockSpec((1, t, kw), lambda bi, hp, qi: (bi, qi, hp)),
            pl.BlockSpec((1, s_len, kw), lambda bi, hp, qi: (bi, 0, hp)),
            pl.BlockSpec((1, s_len, LANES), lambda bi, hp, qi: (bi, 0, hp)),
        ],
        out_specs=pl.BlockSpec((1, t, LANES), lambda bi, hp, qi: (bi, qi, hp)),
        compiler_params=_params("parallel", "parallel", "arbitrary"),
        name=name,
    )(q, k, v)


def _sb_attn_kernel(q_ref, k_ref, v_ref, u_ref, o_ref, *, t):
    qi = pl.program_id(2)
    q = q_ref[0]
    zero = jnp.zeros_like(q)
    qs = (jnp.where(_lane_mask(q.shape, ((0, HEAD_DIM),)), q, zero),
          jnp.where(_lane_mask(q.shape, ((HEAD_DIM, 2 * HEAD_DIM),)), q, zero))
    row = lax.broadcasted_iota(jnp.int32, (t, t), 0)
    col = lax.broadcasted_iota(jnp.int32, (t, t), 1)
    strict = col < row
    u = u_ref[...]

    def block(qx, kb, vb, rest, acc, diag):
        z = _dot_nt(qx, kb)
        lp = jnp.log1p(jnp.exp(-jnp.abs(z)))
        log_beta = jnp.minimum(z, 0.0) - lp
        log_rest = -jnp.maximum(z, 0.0) - lp
        if diag:
            log_rest = jnp.where(strict, log_rest, 0.0)
        later = _dot(log_rest.astype(BF16), u) + rest
        a = jnp.exp(log_beta + later)
        if diag:
            a = jnp.where(strict, a, 0.0)
        return rest + jnp.sum(log_rest, axis=-1, keepdims=True), acc + _dot(a.astype(BF16), vb)

    start = pl.multiple_of(qi * t, t)
    kd = k_ref[0, pl.ds(start, t), :]
    vd = v_ref[0, pl.ds(start, t), :]
    init = tuple(block(qx, kd, vd, jnp.zeros((t, 1), F32), jnp.zeros((t, LANES), F32), True) for qx in qs)

    def body(j, carry):
        off = pl.multiple_of((qi - 1 - j) * t, t)
        kb = k_ref[0, pl.ds(off, t), :]
        vb = v_ref[0, pl.ds(off, t), :]
        return tuple(block(qx, kb, vb, rest, acc, False) for qx, (rest, acc) in zip(qs, carry))

    (_, acca), (_, accb) = lax.fori_loop(0, qi, body, init)
    first_half = _lane_mask(acca.shape, ((0, HEAD_DIM),))
    o_ref[0] = jnp.where(first_half, acca, accb).astype(BF16)


def _sb_attn(q, k, v, u):
    b, s_len, _ = q.shape
    t = u.shape[0]
    return pl.pallas_call(
        functools.partial(_sb_attn_kernel, t=t),
        out_shape=jax.ShapeDtypeStruct((b, s_len, BRANCH_W), BF16),
        grid=(b, 2, s_len // t),
        in_specs=[
            pl.BlockSpec((1, t, LANES), lambda bi, hp, qi: (bi, qi, hp)),
            pl.BlockSpec((1, s_len, LANES), lambda bi, hp, qi: (bi, 0, hp)),
            pl.BlockSpec((1, s_len, LANES), lambda bi, hp, qi: (bi, 0, hp)),
            _const_spec((t, t)),
        ],
        out_specs=pl.BlockSpec((1, t, LANES), lambda bi, hp, qi: (bi, qi, hp)),
        compiler_params=_params("parallel", "parallel", "arbitrary"),
        name="stick_breaking",
    )(q, k, v, u)


def _hgrn_kernel(q_ref, k_ref, lf_ref, v_ref, gate_ref, on_ref, btril_ref, bones_ref, o_ref, st_ref, *, ts):
    @pl.when(pl.program_id(1) == 0)
    def _():
        st_ref[...] = jnp.zeros_like(st_ref)

    c = HGRN_CHUNK
    w = BRANCH_W
    btril = btril_ref[...]
    bones = bones_ref[...]
    head_mask = bones.astype(F32)
    cum = None
    for piece in _split3(lf_ref[0]):
        part = _dot(btril, piece)
        cum = part if cum is None else cum + part

    t_idx = lax.broadcasted_iota(jnp.int32, (c, c, w), 0)
    s_idx = lax.broadcasted_iota(jnp.int32, (c, c, w), 1)
    causal = t_idx >= s_idx
    outs = []
    for ci in range(ts // c):
        r = slice(ci * c, (ci + 1) * c)
        bc = cum[r]
        qc = q_ref[0, r, :].astype(F32)
        kc = k_ref[0, r, :]
        vc = v_ref[0, r, :].astype(F32)
        dec = jnp.exp(jnp.where(causal, bc[:, None, :] - bc[None, :, :], -jnp.inf))
        prod = (qc[:, None, :] * dec * kc[None, :, :]).reshape(c * c, w).astype(BF16)
        scores = _dot(prod, bones).reshape(c, c, w)
        o_intra = jnp.sum(scores * vc[None, :, :], axis=1)
        st = st_ref[...]
        o_inter = _dot_nt((qc * jnp.exp(bc)).astype(BF16), st.astype(BF16))
        b_last = bc[c - 1:c, :]
        k_dec = (kc * jnp.exp(b_last - bc)).astype(BF16)
        upd = _dot(vc.T.astype(BF16), k_dec)
        st_ref[...] = st * jnp.exp(b_last) + upd * head_mask
        outs.append(o_intra + o_inter)
    o = jnp.concatenate(outs, axis=0)
    sq_hi = (o * o).astype(BF16)
    sq_lo = (o * o - sq_hi.astype(F32)).astype(BF16)
    ms = (_dot(sq_hi, bones) + _dot(sq_lo, bones)) * (1.0 / HEAD_DIM)
    y = o * lax.rsqrt(ms + EPS) * on_ref[...] * gate_ref[0].astype(F32)
    o_ref[0] = y.astype(BF16)


def _hgrn(q, k, lf, v, gate, out_norm, btril, bones):
    b, s_len, w = q.shape
    ts = btril.shape[0]
    tok = pl.BlockSpec((1, ts, w), lambda bi, si: (bi, si, 0))
    return pl.pallas_call(
        functools.partial(_hgrn_kernel, ts=ts),
        out_shape=jax.ShapeDtypeStruct((b, s_len, w), BF16),
        grid=(b, s_len // ts),
        in_specs=[tok, tok, tok, tok, tok, _const_spec((1, w)), _const_spec((ts, ts)), _const_spec((w, w))],
        out_specs=tok,
        scratch_shapes=[pltpu.VMEM((w, w), F32)],
        compiler_params=_params("arbitrary", "arbitrary"),
        name="hgrn2",
    )(q, k, lf, v, gate, out_norm.reshape(1, w), btril, bones)


def _merge_kernel(x_ref, ya_ref, yb_ref, yc_ref, yd_ref, g_ref, wgate_ref, wbr_ref, wout_ref, o_ref):
    x = x_ref[...]
    h = _rms(x, g_ref[...]).astype(BF16)
    mixed = None
    for m, y_ref in enumerate((ya_ref, yb_ref, yc_ref, yd_ref)):
        gate = jax.nn.sigmoid(_dot(h, wgate_ref[:, m * D_MODEL:(m + 1) * D_MODEL]))
        term = gate * _dot(y_ref[...], wbr_ref[m])
        mixed = term if mixed is None else mixed + term
    o_ref[...] = x + _dot(mixed.astype(BF16), wout_ref[...])


def _merge(x2, ys, g, wgate, wbr, wout):
    n = x2.shape[0]
    tm = min(TOKEN_TILE, n)
    ytok = pl.BlockSpec((tm, BRANCH_W), lambda i: (i, 0))
    return pl.pallas_call(
        _merge_kernel,
        out_shape=jax.ShapeDtypeStruct((n, D_MODEL), F32),
        grid=(n // tm,),
        in_specs=[pl.BlockSpec((tm, D_MODEL), lambda i: (i, 0)), ytok, ytok, ytok, ytok,
                  _const_spec((1, D_MODEL)), _const_spec((D_MODEL, N_BRANCH * D_MODEL)),
                  _const_spec((N_BRANCH, BRANCH_W, D_MODEL)), _const_spec((D_MODEL, D_MODEL))],
        out_specs=pl.BlockSpec((tm, D_MODEL), lambda i: (i, 0)),
        compiler_params=_params("parallel"),
        name="merge",
    )(x2, *ys, g.reshape(1, D_MODEL), wgate, wbr, wout)


def kernel(x, p, w_in, b_fox_f, mla_q_norm, w_mla_uq, mla_kv_norm, w_mla_ukv, hgrn_lb_logits, hgrn_out_norm,
           w_branch, w_out, ffn_a_wi, ffn_a_wo, ffn_b_wi, ffn_b_wo, w_ple_in, w_ple_gate, norms, final_norm):
    b, s_len, _ = x.shape
    n = b * s_len
    depth = w_in.shape[0]
    tm = min(TOKEN_TILE, s_len)
    t = min(ATTN_TILE, s_len)
    ts = min(HGRN_TILE, s_len)

    tril = jnp.tril(jnp.ones((tm, tm), F32)).astype(BF16)
    selq, selk, rowq, rowk = _fox_selectors()
    rope = _rope_tables(s_len)
    suffix = jnp.tril(jnp.ones((t, t), F32), -1).astype(BF16)
    chunk_id = np.arange(ts) // HGRN_CHUNK
    btril = jnp.asarray((chunk_id[:, None] == chunk_id[None, :]) & (np.arange(ts)[:, None] >= np.arange(ts)[None, :]),
                        BF16)
    head_id = np.arange(BRANCH_W) // HEAD_DIM
    bones = jnp.asarray(head_id[:, None] == head_id[None, :], BF16)

    lb_cum = jnp.cumsum(jax.nn.softmax(hgrn_lb_logits.astype(F32), axis=0), axis=0)
    lower_bounds = lb_cum - lb_cum[0]

    fox_a = ((0, HEAD_DIM), (LANES, LANES + 8))
    fox_b = ((HEAD_DIM, LANES), (LANES + 8, LANES + 16))
    mla_a = ((0, LANES),)
    mla_b = ((LANES, 2 * LANES),)

    x2 = x.reshape(n, D_MODEL)
    for i in range(depth):
        x2 = _ffn(x2, norms[i, 0], ffn_a_wi[i], ffn_a_wo[i])
        w, bf, uqa, uqb, ukvk, ukvv = _pre_weights(w_in[i], b_fox_f[i], w_mla_uq[i], w_mla_ukv[i])
        (fq, fk, fv, mq, mk, mv, sq, sk, sv, hq, hk, hlf, hv, hg) = _pre(
            x2.reshape(b, s_len, D_MODEL), norms[i, 1], w, bf, uqa, uqb, ukvk, ukvv,
            mla_q_norm[i], mla_kv_norm[i], lower_bounds[i], (tril, selq, selk, rowq, rowk, rope))
        y_a = _softmax_attn(fq, fk, fv, 2 * LANES, fox_a, fox_b, "fox")
        y_b = _softmax_attn(mq, mk, mv, 2 * LANES, mla_a, mla_b, "mla")
        y_c = _sb_attn(sq, sk, sv, suffix)
        y_d = _hgrn(hq, hk, hlf, hv, hg, hgrn_out_norm[i], btril, bones)
        ys = [y.reshape(n, BRANCH_W) for y in (y_a, y_b, y_c, y_d)]
        x2 = _merge(x2, ys, norms[i, 1], w_in[i][:, IN_OFFS[14]:].astype(BF16),
                    w_branch[i].astype(BF16), w_out[i].astype(BF16))
        x2 = _ffn(x2, norms[i, 2], ffn_b_wi[i], ffn_b_wo[i])
        x2 = _ple(x2, p[i].reshape(n, PLE_DIM), norms[i, 3], w_ple_gate[i], w_ple_in[i],
                  final_norm if i == depth - 1 else None)
    return x2.reshape(b, s_len, D_MODEL)
```

```python
import functools

import jax
import jax.numpy as jnp
import numpy as np
from jax import lax
from jax.experimental import pallas as pl
from jax.experimental.pallas import tpu as pltpu

F32 = jnp.float32
BF16 = jnp.bfloat16

D_MODEL = 1024
PLE_DIM = 256
D_FF = 2816
EPS = 1e-6
LOG2E = 1.4426950408889634
N_BRANCH = 4
N_HEADS = 4
HEAD_DIM = 64
BRANCH_W = 256
MLA_NOPE = 64
MLA_ROPE = 32
MLA_Q_RANK = 256
MLA_KV_RANK = 128
ROPE_BASE = 10000.0
HGRN_CHUNK = 32
LANES = 128

IN_SPLITS = (256, 256, 256, 4, 256, 128, 32, 256, 256, 256, 256, 256, 256, 256, 4096)
IN_OFFS = tuple(int(o) for o in np.cumsum((0,) + IN_SPLITS))

PRE_COLS = dict(
    fq=(0, 256), fk=(256, 512), fv=(512, 768), ff=(768, 896),
    mcq=(896, 1152), mckv=(1152, 1280), mkr=(1280, 1408), mkr_sw=(1408, 1536),
    sq=(1536, 1792), sk=(1792, 2048), sv=(2048, 2304),
    hq=(2304, 2560), hf=(2560, 2816), hi=(2816, 3072), hg=(3072, 3328),
)
PRE_W = 3328

FF_CHUNK = 256
TOKEN_TILE = 512
ATTN_TILE = 1024
SOFTMAX_ROWS = 256
SB_KEY_TILE = 256
HGRN_TILE = 256
VMEM_LIMIT = 56 * 1024 * 1024


def _dot(a, b):
    return jnp.dot(a, b, preferred_element_type=F32)


def _dot_nt(a, b):
    return lax.dot_general(a, b, (((1,), (1,)), ((), ())), preferred_element_type=F32)


def _rms(x, g):
    return x * lax.rsqrt(jnp.mean(x * x, axis=-1, keepdims=True) + EPS) * g


def _log_sigmoid(x):
    return jnp.minimum(x, 0.0) - jnp.log1p(jnp.exp(-jnp.abs(x)))


def _split3(x):
    hi = x.astype(BF16)
    r1 = x - hi.astype(F32)
    mid = r1.astype(BF16)
    lo = (r1 - mid.astype(F32)).astype(BF16)
    return hi, mid, lo


def _params(*sem):
    return pltpu.CompilerParams(dimension_semantics=sem, vmem_limit_bytes=VMEM_LIMIT)


def _const_spec(shape):
    nd = len(shape)
    return pl.BlockSpec(shape, lambda *_: (0,) * nd, pipeline_mode=pl.Buffered(1))


def _ffn_kernel(x_ref, g_ref, wg_ref, wu_ref, wo_ref, o_ref, acc_ref):
    x = x_ref[...]
    h = _rms(x, g_ref[...]).astype(BF16)
    for c in range(D_FF // FF_CHUNK):
        sl = slice(c * FF_CHUNK, (c + 1) * FF_CHUNK)
        g = _dot(h, wg_ref[:, sl])
        u = _dot(h, wu_ref[:, sl])
        a = (g * jax.nn.sigmoid(g) * u).astype(BF16)
        part = _dot(a, wo_ref[sl, :])
        if c == 0:
            acc_ref[...] = part
        else:
            acc_ref[...] += part
    o_ref[...] = x + 0.5 * acc_ref[...]


def _ffn(x2, g, wi, wo):
    n = x2.shape[0]
    tm = min(TOKEN_TILE, n)
    wg = wi[:, :D_FF].astype(BF16)
    wu = wi[:, D_FF:].astype(BF16)
    return pl.pallas_call(
        _ffn_kernel,
        out_shape=jax.ShapeDtypeStruct((n, D_MODEL), F32),
        grid=(n // tm,),
        in_specs=[
            pl.BlockSpec((tm, D_MODEL), lambda i: (i, 0)),
            _const_spec((1, D_MODEL)),
            _const_spec((D_MODEL, D_FF)),
            _const_spec((D_MODEL, D_FF)),
            _const_spec((D_FF, D_MODEL)),
        ],
        out_specs=pl.BlockSpec((tm, D_MODEL), lambda i: (i, 0)),
        scratch_shapes=[pltpu.VMEM((tm, D_MODEL), F32)],
        compiler_params=_params("parallel"),
        name="ffn",
    )(x2, g.reshape(1, D_MODEL), wg, wu, wo.astype(BF16))


def _ple_kernel(x_ref, p_ref, g_ref, wpg_ref, wpe_ref, *rest, final):
    o_ref = rest[-1]
    x = x_ref[...]
    h = _rms(x, g_ref[...]).astype(BF16)
    gate = jax.nn.sigmoid(_dot(h, wpg_ref[...]))
    y = x + gate * _dot(p_ref[...].astype(BF16), wpe_ref[...])
    if final:
        y = _rms(y, rest[0][...])
    o_ref[...] = y


def _ple(x2, p2, g, wpg, wpe, final_g=None):
    n = x2.shape[0]
    tm = min(TOKEN_TILE, n)
    final = final_g is not None
    in_specs = [
        pl.BlockSpec((tm, D_MODEL), lambda i: (i, 0)),
        pl.BlockSpec((tm, PLE_DIM), lambda i: (i, 0)),
        _const_spec((1, D_MODEL)),
        _const_spec((D_MODEL, D_MODEL)),
        _const_spec((PLE_DIM, D_MODEL)),
    ]
    args = [x2, p2, g.reshape(1, D_MODEL), wpg.astype(BF16), wpe.astype(BF16)]
    if final:
        in_specs.append(_const_spec((1, D_MODEL)))
        args.append(final_g.reshape(1, D_MODEL))
    return pl.pallas_call(
        functools.partial(_ple_kernel, final=final),
        out_shape=jax.ShapeDtypeStruct((n, D_MODEL), F32),
        grid=(n // tm,),
        in_specs=in_specs,
        out_specs=pl.BlockSpec((tm, D_MODEL), lambda i: (i, 0)),
        compiler_params=_params("parallel"),
        name="ple_final" if final else "ple",
    )(*args)


def _pre_kernel(x_ref, g_ref, w_ref, bf_ref, tril_ref, selq_ref, selk_ref, rowq_ref, rowk_ref,
                qn_ref, kvn_ref, wuqa_ref, wuqb_ref, wukvk_ref, wukvv_ref, rope_ref, lb_ref,
                fq_ref, fk_ref, fv_ref, mq_ref, mk_ref, mv_ref, sq_ref, sk_ref, sv_ref,
                hq_ref, hk_ref, hlf_ref, hv_ref, hg_ref, carry_ref):
    @pl.when(pl.program_id(1) == 0)
    def _():
        carry_ref[...] = jnp.zeros_like(carry_ref)

    h = _rms(x_ref[0], g_ref[...]).astype(BF16)

    def proj(name):
        lo, hi = PRE_COLS[name]
        return _dot(h, w_ref[:, lo:hi])

    log_f = _log_sigmoid(proj("ff") + bf_ref[...])
    tril = tril_ref[...]
    c = carry_ref[...]
    for piece in _split3(log_f):
        c = c + _dot(tril, piece)
    carry_ref[...] = c[c.shape[0] - 1:, :]
    c_parts = jnp.concatenate(_split3(c * LOG2E), axis=-1)
    cq = (_dot(c_parts, selq_ref[...]) + rowq_ref[...]).astype(BF16)
    ck = (_dot(c_parts, selk_ref[...]) + rowk_ref[...]).astype(BF16)
    fq = (proj("fq") * LOG2E).astype(BF16)
    fk = proj("fk").astype(BF16)
    for p in range(2):
        pair = slice(p * LANES, (p + 1) * LANES)
        fq_ref[0, :, 2 * p * LANES:(2 * p + 1) * LANES] = fq[:, pair]
        fq_ref[0, :, (2 * p + 1) * LANES:(2 * p + 2) * LANES] = cq[:, pair]
        fk_ref[0, :, 2 * p * LANES:(2 * p + 1) * LANES] = fk[:, pair]
        fk_ref[0, :, (2 * p + 1) * LANES:(2 * p + 2) * LANES] = ck[:, pair]
    fv_ref[0] = proj("fv").astype(BF16)

    rope = rope_ref[...]
    cos_q, sin_q = rope[:, 0:LANES], rope[:, LANES:2 * LANES]
    cos_k, sin_k = rope[:, 2 * LANES:3 * LANES], rope[:, 3 * LANES:4 * LANES]
    cq_lat = _rms(proj("mcq"), qn_ref[...]).astype(BF16)
    qa = _dot(cq_lat, wuqa_ref[...])
    qb = _dot(cq_lat, wuqb_ref[...])
    ckv_lat = _rms(proj("mckv"), kvn_ref[...]).astype(BF16)
    k_nope = _dot(ckv_lat, wukvk_ref[...])
    k_rope = proj("mkr") * cos_k + proj("mkr_sw") * sin_k
    for hd in range(N_HEADS):
        sl = slice(hd * LANES, (hd + 1) * LANES)
        mq_ref[0, :, sl] = (qa[:, sl] * cos_q + qb[:, sl] * sin_q).astype(BF16)
        mk_ref[0, :, sl] = (k_nope[:, sl] + k_rope).astype(BF16)
    mv_ref[0] = _dot(ckv_lat, wukvv_ref[...]).astype(BF16)

    sq_ref[0] = (proj("sq") * LOG2E).astype(BF16)
    sk_ref[0] = proj("sk").astype(BF16)
    sv_ref[0] = proj("sv").astype(BF16)

    lb = lb_ref[...]
    f = lb + (1.0 - lb) * jax.nn.sigmoid(proj("hf"))
    hq_ref[0] = proj("hq").astype(BF16)
    hk_ref[0] = 1.0 - f
    hlf_ref[0] = jnp.log(f)
    hv_ref[0] = proj("hi").astype(BF16)
    hg = proj("hg")
    hg_ref[0] = (hg * jax.nn.sigmoid(hg)).astype(BF16)


def _pre_weights(w_in, b_f, w_uq, w_ukv):
    cols = lambda j: w_in[:, IN_OFFS[j]:IN_OFFS[j + 1]]
    zeros = lambda n: jnp.zeros((D_MODEL, n), F32)
    half = MLA_ROPE // 2
    mkr = cols(6)
    mkr_sw = jnp.concatenate([-mkr[:, half:], mkr[:, :half]], axis=1)
    pad_rope = lambda t: jnp.concatenate([zeros(MLA_NOPE), t, zeros(LANES - MLA_NOPE - MLA_ROPE)], axis=1)
    scale = HEAD_DIM ** -0.5
    w = jnp.concatenate([
        cols(0) * scale, cols(1), cols(2),
        jnp.concatenate([cols(3), zeros(LANES - N_HEADS)], axis=1),
        cols(4), cols(5), pad_rope(mkr), pad_rope(mkr_sw),
        cols(7) * scale, cols(8), cols(9),
        cols(10), cols(11), cols(12), cols(13)], axis=1).astype(BF16)
    bf = jnp.concatenate([b_f, jnp.zeros((LANES - N_HEADS,), F32)]).reshape(1, LANES)

    dq = MLA_NOPE + MLA_ROPE
    uq = w_uq.reshape(MLA_Q_RANK, N_HEADS, dq)
    zq = jnp.zeros((MLA_Q_RANK, N_HEADS, LANES - dq), F32)
    uqa = jnp.concatenate([uq, zq], axis=2).reshape(MLA_Q_RANK, N_HEADS * LANES)
    rope_cols = uq[:, :, MLA_NOPE:]
    rope_sw = jnp.concatenate([-rope_cols[:, :, half:], rope_cols[:, :, :half]], axis=2)
    uqb = jnp.concatenate([jnp.zeros((MLA_Q_RANK, N_HEADS, MLA_NOPE), F32), rope_sw, zq],
                          axis=2).reshape(MLA_Q_RANK, N_HEADS * LANES)
    ukv = w_ukv.reshape(MLA_KV_RANK, N_HEADS, MLA_NOPE + HEAD_DIM)
    ukvk = jnp.concatenate([ukv[:, :, :MLA_NOPE], jnp.zeros((MLA_KV_RANK, N_HEADS, LANES - MLA_NOPE), F32)],
                           axis=2).reshape(MLA_KV_RANK, N_HEADS * LANES)
    ukvv = ukv[:, :, MLA_NOPE:].reshape(MLA_KV_RANK, N_HEADS * HEAD_DIM)
    return w, bf, uqa.astype(BF16), uqb.astype(BF16), ukvk.astype(BF16), ukvv.astype(BF16)


def _fox_selectors():
    selq = np.zeros((3 * LANES, 2 * LANES), np.float32)
    selk = np.zeros((3 * LANES, 2 * LANES), np.float32)
    rowq = np.zeros((1, 2 * LANES), np.float32)
    rowk = np.zeros((1, 2 * LANES), np.float32)
    for hd in range(N_HEADS):
        base = (hd // 2) * LANES + (hd % 2) * 8
        for piece in range(3):
            selq[piece * LANES + hd, base + piece] = 1.0
            selk[piece * LANES + hd, base + 3 + piece] = -1.0
            rowq[0, base + 3 + piece] = 1.0
            rowk[0, base + piece] = 1.0
    return (jnp.asarray(selq, BF16), jnp.asarray(selk, BF16), jnp.asarray(rowq), jnp.asarray(rowk))


def _rope_tables(s_len):
    half = MLA_ROPE // 2
    inv = ROPE_BASE ** (-jnp.arange(half, dtype=F32) / half)
    ang = jnp.arange(s_len).astype(F32)[:, None] * inv[None, :]
    cos, sin = jnp.cos(ang), jnp.sin(ang)
    tail = jnp.zeros((s_len, LANES - MLA_NOPE - MLA_ROPE), F32)
    cos_t = jnp.concatenate([jnp.ones((s_len, MLA_NOPE), F32), cos, cos, tail], axis=1)
    sin_t = jnp.concatenate([jnp.zeros((s_len, MLA_NOPE), F32), sin, sin, tail], axis=1)
    scale = (MLA_NOPE + MLA_ROPE) ** -0.5 * LOG2E
    return jnp.concatenate([cos_t * scale, sin_t * scale, cos_t, sin_t], axis=1)


def _pre(x3, g, w, bf, uqa, uqb, ukvk, ukvv, qn, kvn, lb, consts):
    b, s_len, _ = x3.shape
    tm = min(TOKEN_TILE, s_len)
    tril, selq, selk, rowq, rowk, rope = consts
    tok = lambda width: pl.BlockSpec((1, tm, width), lambda bi, si: (bi, si, 0))
    bf16_out = lambda width: jax.ShapeDtypeStruct((b, s_len, width), BF16)
    f32_out = lambda width: jax.ShapeDtypeStruct((b, s_len, width), F32)
    in_specs = [
        tok(D_MODEL), _const_spec((1, D_MODEL)), _const_spec((D_MODEL, PRE_W)), _const_spec((1, LANES)),
        _const_spec((tm, tm)), _const_spec(selq.shape), _const_spec(selk.shape),
        _const_spec(rowq.shape), _const_spec(rowk.shape),
        _const_spec((1, MLA_Q_RANK)), _const_spec((1, MLA_KV_RANK)),
        _const_spec(uqa.shape), _const_spec(uqb.shape), _const_spec(ukvk.shape), _const_spec(ukvv.shape),
        pl.BlockSpec((tm, 4 * LANES), lambda bi, si: (si, 0)),
        _const_spec((1, BRANCH_W)),
    ]
    widths = [(512, BF16), (512, BF16), (256, BF16), (512, BF16), (512, BF16), (256, BF16),
              (256, BF16), (256, BF16), (256, BF16),
              (256, BF16), (256, F32), (256, F32), (256, BF16), (256, BF16)]
    return pl.pallas_call(
        _pre_kernel,
        out_shape=[bf16_out(wd) if dt == BF16 else f32_out(wd) for wd, dt in widths],
        grid=(b, s_len // tm),
        in_specs=in_specs,
        out_specs=[tok(wd) for wd, _ in widths],
        scratch_shapes=[pltpu.VMEM((1, LANES), F32)],
        compiler_params=_params("arbitrary", "arbitrary"),
        name="pre_mix",
    )(x3, g.reshape(1, D_MODEL), w, bf, tril, selq, selk, rowq, rowk,
      qn.reshape(1, -1), kvn.reshape(1, -1), uqa, uqb, ukvk, ukvv, rope, lb.reshape(1, -1))


def _lane_mask(shape, ranges):
    lane = lax.broadcasted_iota(jnp.int32, shape, 1)
    m = None
    for lo, hi in ranges:
        r = (lane >= lo) & (lane < hi)
        m = r if m is None else (m | r)
    return m


def _softmax_attn_kernel(q_ref, k_ref, v_ref, o_ref, qm_ref, s_ref, m_ref, l_ref, acc_ref, *, t, lanes_a, lanes_b):
    qi = pl.program_id(2)
    q = q_ref[0]
    zero = jnp.zeros_like(q)
    qm_ref[0] = jnp.where(_lane_mask(q.shape, lanes_a), q, zero)
    qm_ref[1] = jnp.where(_lane_mask(q.shape, lanes_b), q, zero)
    m_ref[...] = jnp.full(m_ref.shape, -1e30, F32)
    l_ref[...] = jnp.zeros(l_ref.shape, F32)
    acc_ref[...] = jnp.zeros(acc_ref.shape, F32)

    def logits(j, slot):
        kb = k_ref[0, pl.ds(pl.multiple_of(j * t, t), t), :]
        for hd in range(2):
            s_ref[slot, hd] = _dot_nt(qm_ref[hd], kb)

    def update(j, slot, diagonal):
        vb = v_ref[0, pl.ds(pl.multiple_of(j * t, t), t), :]
        for hd in range(2):
            s = s_ref[slot, hd]
            if diagonal:
                row = lax.broadcasted_iota(jnp.int32, (t, t), 0)
                col = lax.broadcasted_iota(jnp.int32, (t, t), 1)
                s = jnp.where(col <= row, s, -1e30)
            m = m_ref[hd]
            m_new = jnp.maximum(m, jnp.max(s, axis=-1, keepdims=True))
            alpha = jnp.exp2(m - m_new)
            p = jnp.exp2(s - jnp.concatenate([m_new] * (t // LANES), axis=1))
            m_ref[hd] = m_new
            l_ref[hd] = alpha * l_ref[hd] + jnp.sum(p, axis=-1, keepdims=True)
            acc_ref[hd] = alpha * acc_ref[hd] + _dot(p.astype(BF16), vb)

    logits(0, 0)

    def body(j, carry):
        slot = lax.rem(j, 2)
        update(j, slot, False)
        logits(j + 1, 1 - slot)
        return carry

    lax.fori_loop(0, qi, body, 0)
    update(qi, lax.rem(qi, 2), True)
    first_half = _lane_mask((t, LANES), ((0, HEAD_DIM),))
    o_ref[0] = jnp.where(first_half, acc_ref[0] / l_ref[0], acc_ref[1] / l_ref[1]).astype(BF16)


def _softmax_attn(q, k, v, kw, lanes_a, lanes_b, name):
    b, s_len, _ = q.shape
    t = min(ATTN_TILE, s_len)
    return pl.pallas_call(
        functools.partial(_softmax_attn_kernel, t=t, lanes_a=lanes_a, lanes_b=lanes_b),
        out_shape=jax.ShapeDtypeStruct((b, s_len, BRANCH_W), BF16),
        grid=(b, 2, s_len // t),
        in_specs=[
            pl.BlockSpec((1, t, kw), lambda bi, hp, qi: (bi, qi, hp)),
            pl.BlockSpec((1, s_len, kw), lambda bi, hp, qi: (bi, 0, hp)),
            pl.BlockSpec((1, s_len, LANES), lambda bi, hp, qi: (bi, 0, hp)),
        ],
        out_specs=pl.BlockSpec((1, t, LANES), lambda bi, hp, qi: (bi, qi, hp)),
        scratch_shapes=[pltpu.VMEM((2, t, kw), BF16), pltpu.VMEM((2, 2, t, t), F32),
                        pltpu.VMEM((2, t, LANES), F32), pltpu.VMEM((2, t, LANES), F32),
                        pltpu.VMEM((2, t, LANES), F32)],
        compiler_params=_params("parallel", "parallel", "arbitrary"),
        name=name,
    )(q, k, v)


def _sb_attn_kernel(q_ref, k_ref, v_ref, u_ref, o_ref, qm_ref, rest_ref, acc_ref, *, tk):
    tq = 2 * tk
    qi = pl.program_id(2)
    q = q_ref[0]
    zero = jnp.zeros_like(q)
    qm_ref[0] = jnp.where(_lane_mask(q.shape, ((0, HEAD_DIM),)), q, zero)
    qm_ref[1] = jnp.where(_lane_mask(q.shape, ((HEAD_DIM, 2 * HEAD_DIM),)), q, zero)
    rest_ref[...] = jnp.zeros(rest_ref.shape, F32)
    acc_ref[...] = jnp.zeros(acc_ref.shape, F32)

    def block(hd, r0, nrows, blk, diag):
        rows = pl.ds(r0, nrows)
        off = pl.multiple_of(blk * tk, tk)
        z = _dot_nt(qm_ref[hd, rows, :], k_ref[0, pl.ds(off, tk), :])
        log_beta = jnp.minimum(z, 0.0) - jnp.log2(1.0 + jnp.exp2(-jnp.abs(z)))
        log_rest = log_beta - z
        if diag:
            row = lax.broadcasted_iota(jnp.int32, (tk, tk), 0)
            col = lax.broadcasted_iota(jnp.int32, (tk, tk), 1)
            strict = col < row
            log_rest = jnp.where(strict, log_rest, 0.0)
        rest = rest_ref[hd, rows, :]
        later = _dot(log_rest.astype(BF16), u_ref[...]) + jnp.concatenate([rest] * (tk // LANES), axis=1)
        a = jnp.exp2(log_beta + later)
        if diag:
            a = jnp.where(strict, a, 0.0)
        rest_ref[hd, rows, :] = rest + jnp.sum(log_rest, axis=-1, keepdims=True)
        acc_ref[hd, rows, :] += _dot(a.astype(BF16), v_ref[0, pl.ds(off, tk), :])

    for hd in range(2):
        block(hd, tk, tk, 2 * qi + 1, True)
        block(hd, tk, tk, 2 * qi, False)
        block(hd, 0, tk, 2 * qi, True)

    def body(j, carry):
        chunk = qi - 1 - j
        for hd in range(2):
            block(hd, 0, tq, 2 * chunk + 1, False)
            block(hd, 0, tq, 2 * chunk, False)
        return carry

    lax.fori_loop(0, qi, body, 0)
    first_half = _lane_mask((tq, LANES), ((0, HEAD_DIM),))
    o_ref[0] = jnp.where(first_half, acc_ref[0], acc_ref[1]).astype(BF16)


def _sb_attn(q, k, v, u):
    b, s_len, _ = q.shape
    tk = u.shape[0]
    tq = 2 * tk
    return pl.pallas_call(
        functools.partial(_sb_attn_kernel, tk=tk),
        out_shape=jax.ShapeDtypeStruct((b, s_len, BRANCH_W), BF16),
        grid=(b, 2, s_len // tq),
        in_specs=[
            pl.BlockSpec((1, tq, LANES), lambda bi, hp, qi: (bi, qi, hp)),
            pl.BlockSpec((1, s_len, LANES), lambda bi, hp, qi: (bi, 0, hp)),
            pl.BlockSpec((1, s_len, LANES), lambda bi, hp, qi: (bi, 0, hp)),
            _const_spec((tk, tk)),
        ],
        out_specs=pl.BlockSpec((1, tq, LANES), lambda bi, hp, qi: (bi, qi, hp)),
        scratch_shapes=[pltpu.VMEM((2, tq, LANES), BF16), pltpu.VMEM((2, tq, LANES), F32),
                        pltpu.VMEM((2, tq, LANES), F32)],
        compiler_params=_params("parallel", "parallel", "arbitrary"),
        name="stick_breaking",
    )(q, k, v, u)


def _hgrn_kernel(q_ref, k_ref, lf_ref, v_ref, gate_ref, on_ref, btril_ref, bones_ref, o_ref, st_ref, *, ts):
    @pl.when(pl.program_id(1) == 0)
    def _():
        st_ref[...] = jnp.zeros_like(st_ref)

    c = HGRN_CHUNK
    w = BRANCH_W
    btril = btril_ref[...]
    bones = bones_ref[...]
    head_mask = bones.astype(F32)
    cum = None
    for piece in _split3(lf_ref[0]):
        part = _dot(btril, piece)
        cum = part if cum is None else cum + part

    t_idx = lax.broadcasted_iota(jnp.int32, (c, c, w), 0)
    s_idx = lax.broadcasted_iota(jnp.int32, (c, c, w), 1)
    causal = t_idx >= s_idx
    outs = []
    for ci in range(ts // c):
        r = slice(ci * c, (ci + 1) * c)
        bc = cum[r]
        qc = q_ref[0, r, :].astype(F32)
        kc = k_ref[0, r, :]
        vc = v_ref[0, r, :].astype(F32)
        dec = jnp.exp(jnp.where(causal, bc[:, None, :] - bc[None, :, :], -jnp.inf))
        prod = (qc[:, None, :] * dec * kc[None, :, :]).reshape(c * c, w).astype(BF16)
        scores = _dot(prod, bones).reshape(c, c, w)
        o_intra = jnp.sum(scores * vc[None, :, :], axis=1)
        st = st_ref[...]
        o_inter = _dot_nt((qc * jnp.exp(bc)).astype(BF16), st.astype(BF16))
        b_last = bc[c - 1:c, :]
        k_dec = (kc * jnp.exp(b_last - bc)).astype(BF16)
        upd = _dot(vc.T.astype(BF16), k_dec)
        st_ref[...] = st * jnp.exp(b_last) + upd * head_mask
        outs.append(o_intra + o_inter)
    o = jnp.concatenate(outs, axis=0)
    sq_hi = (o * o).astype(BF16)
    sq_lo = (o * o - sq_hi.astype(F32)).astype(BF16)
    ms = (_dot(sq_hi, bones) + _dot(sq_lo, bones)) * (1.0 / HEAD_DIM)
    y = o * lax.rsqrt(ms + EPS) * on_ref[...] * gate_ref[0].astype(F32)
    o_ref[0] = y.astype(BF16)


def _hgrn(q, k, lf, v, gate, out_norm, btril, bones):
    b, s_len, w = q.shape
    ts = btril.shape[0]
    tok = pl.BlockSpec((1, ts, w), lambda bi, si: (bi, si, 0))
    return pl.pallas_call(
        functools.partial(_hgrn_kernel, ts=ts),
        out_shape=jax.ShapeDtypeStruct((b, s_len, w), BF16),
        grid=(b, s_len // ts),
        in_specs=[tok, tok, tok, tok, tok, _const_spec((1, w)), _const_spec((ts, ts)), _const_spec((w, w))],
        out_specs=tok,
        scratch_shapes=[pltpu.VMEM((w, w), F32)],
        compiler_params=_params("arbitrary", "arbitrary"),
        name="hgrn2",
    )(q, k, lf, v, gate, out_norm.reshape(1, w), btril, bones)


def _merge_kernel(x_ref, ya_ref, yb_ref, yc_ref, yd_ref, g_ref, wgate_ref, wbr_ref, wout_ref, o_ref):
    x = x_ref[...]
    h = _rms(x, g_ref[...]).astype(BF16)
    mixed = None
    for m, y_ref in enumerate((ya_ref, yb_ref, yc_ref, yd_ref)):
        gate = jax.nn.sigmoid(_dot(h, wgate_ref[:, m * D_MODEL:(m + 1) * D_MODEL]))
        term = gate * _dot(y_ref[...], wbr_ref[m])
        mixed = term if mixed is None else mixed + term
    o_ref[...] = x + _dot(mixed.astype(BF16), wout_ref[...])


def _merge(x2, ys, g, wgate, wbr, wout):
    n = x2.shape[0]
    tm = min(TOKEN_TILE, n)
    ytok = pl.BlockSpec((tm, BRANCH_W), lambda i: (i, 0))
    return pl.pallas_call(
        _merge_kernel,
        out_shape=jax.ShapeDtypeStruct((n, D_MODEL), F32),
        grid=(n // tm,),
        in_specs=[pl.BlockSpec((tm, D_MODEL), lambda i: (i, 0)), ytok, ytok, ytok, ytok,
                  _const_spec((1, D_MODEL)), _const_spec((D_MODEL, N_BRANCH * D_MODEL)),
                  _const_spec((N_BRANCH, BRANCH_W, D_MODEL)), _const_spec((D_MODEL, D_MODEL))],
        out_specs=pl.BlockSpec((tm, D_MODEL), lambda i: (i, 0)),
        compiler_params=_params("parallel"),
        name="merge",
    )(x2, *ys, g.reshape(1, D_MODEL), wgate, wbr, wout)


def kernel(x, p, w_in, b_fox_f, mla_q_norm, w_mla_uq, mla_kv_norm, w_mla_ukv, hgrn_lb_logits, hgrn_out_norm,
           w_branch, w_out, ffn_a_wi, ffn_a_wo, ffn_b_wi, ffn_b_wo, w_ple_in, w_ple_gate, norms, final_norm):
    b, s_len, _ = x.shape
    n = b * s_len
    depth = w_in.shape[0]
    tm = min(TOKEN_TILE, s_len)
    t = min(SB_KEY_TILE, s_len // 2)
    ts = min(HGRN_TILE, s_len)

    tril = jnp.tril(jnp.ones((tm, tm), F32)).astype(BF16)
    selq, selk, rowq, rowk = _fox_selectors()
    rope = _rope_tables(s_len)
    suffix = jnp.tril(jnp.ones((t, t), F32), -1).astype(BF16)
    chunk_id = np.arange(ts) // HGRN_CHUNK
    btril = jnp.asarray((chunk_id[:, None] == chunk_id[None, :]) & (np.arange(ts)[:, None] >= np.arange(ts)[None, :]),
                        BF16)
    head_id = np.arange(BRANCH_W) // HEAD_DIM
    bones = jnp.asarray(head_id[:, None] == head_id[None, :], BF16)

    lb_cum = jnp.cumsum(jax.nn.softmax(hgrn_lb_logits.astype(F32), axis=0), axis=0)
    lower_bounds = lb_cum - lb_cum[0]

    fox_a = ((0, HEAD_DIM), (LANES, LANES + 8))
    fox_b = ((HEAD_DIM, LANES), (LANES + 8, LANES + 16))
    mla_a = ((0, LANES),)
    mla_b = ((LANES, 2 * LANES),)

    x2 = x.reshape(n, D_MODEL)
    for i in range(depth):
        x2 = _ffn(x2, norms[i, 0], ffn_a_wi[i], ffn_a_wo[i])
        w, bf, uqa, uqb, ukvk, ukvv = _pre_weights(w_in[i], b_fox_f[i], w_mla_uq[i], w_mla_ukv[i])
        (fq, fk, fv, mq, mk, mv, sq, sk, sv, hq, hk, hlf, hv, hg) = _pre(
            x2.reshape(b, s_len, D_MODEL), norms[i, 1], w, bf, uqa, uqb, ukvk, ukvv,
            mla_q_norm[i], mla_kv_norm[i], lower_bounds[i], (tril, selq, selk, rowq, rowk, rope))
        y_a = _softmax_attn(fq, fk, fv, 2 * LANES, fox_a, fox_b, "fox")
        y_b = _softmax_attn(mq, mk, mv, 2 * LANES, mla_a, mla_b, "mla")
        y_c = _sb_attn(sq, sk, sv, suffix)
        y_d = _hgrn(hq, hk, hlf, hv, hg, hgrn_out_norm[i], btril, bones)
        ys = [y.reshape(n, BRANCH_W) for y in (y_a, y_b, y_c, y_d)]
        x2 = _merge(x2, ys, norms[i, 1], w_in[i][:, IN_OFFS[14]:].astype(BF16),
                    w_branch[i].astype(BF16), w_out[i].astype(BF16))
        x2 = _ffn(x2, norms[i, 2], ffn_b_wi[i], ffn_b_wo[i])
        x2 = _ple(x2, p[i].reshape(n, PLE_DIM), norms[i, 3], w_ple_gate[i], w_ple_in[i],
                  final_norm if i == depth - 1 else None)
    return x2.reshape(b, s_len, D_MODEL)
```

```python
import functools

import jax
import jax.numpy as jnp
import numpy as np
from jax import lax
from jax.experimental import pallas as pl
from jax.experimental.pallas import tpu as pltpu

F32 = jnp.float32
BF16 = jnp.bfloat16

D_MODEL = 1024
PLE_DIM = 256
D_FF = 2816
EPS = 1e-6
LOG2E = 1.4426950408889634
N_BRANCH = 4
N_HEADS = 4
HEAD_DIM = 64
BRANCH_W = 256
MLA_NOPE = 64
MLA_ROPE = 32
MLA_Q_RANK = 256
MLA_KV_RANK = 128
ROPE_BASE = 10000.0
HGRN_CHUNK = 32
LANES = 128

IN_SPLITS = (256, 256, 256, 4, 256, 128, 32, 256, 256, 256, 256, 256, 256, 256, 4096)
IN_OFFS = tuple(int(o) for o in np.cumsum((0,) + IN_SPLITS))

PRE_COLS = dict(
    fq=(0, 256), fk=(256, 512), fv=(512, 768), ff_mckv=(768, 1024),
    mcq=(1024, 1280), mkr_pair=(1280, 1536),
    sq=(1536, 1792), sk=(1792, 2048), sv=(2048, 2304),
    hq=(2304, 2560), hf=(2560, 2816), hi=(2816, 3072), hg=(3072, 3328),
)
PRE_W = 3328

FF_CHUNK = 256
TOKEN_TILE = 512
ATTN_TILE = 1024
SOFTMAX_ROWS = 256
SB_KEY_TILE = 256
SB_SUBBLOCKS = 4
HGRN_TILE = 256
VMEM_LIMIT = 56 * 1024 * 1024


def _dot(a, b):
    return jnp.dot(a, b, preferred_element_type=F32)


def _dot_nt(a, b):
    return lax.dot_general(a, b, (((1,), (1,)), ((), ())), preferred_element_type=F32)


def _rms(x, g):
    return x * lax.rsqrt(jnp.mean(x * x, axis=-1, keepdims=True) + EPS) * g


def _log_sigmoid(x):
    return jnp.minimum(x, 0.0) - jnp.log1p(jnp.exp(-jnp.abs(x)))


def _split3(x):
    hi = x.astype(BF16)
    r1 = x - hi.astype(F32)
    mid = r1.astype(BF16)
    lo = (r1 - mid.astype(F32)).astype(BF16)
    return hi, mid, lo


def _params(*sem):
    return pltpu.CompilerParams(dimension_semantics=sem, vmem_limit_bytes=VMEM_LIMIT)


def _const_spec(shape):
    nd = len(shape)
    return pl.BlockSpec(shape, lambda *_: (0,) * nd, pipeline_mode=pl.Buffered(1))


def _ffn_kernel(x_ref, g_ref, wg_ref, wu_ref, wo_ref, o_ref, acc_ref):
    x = x_ref[...]
    h = _rms(x, g_ref[...]).astype(BF16)
    for c in range(D_FF // FF_CHUNK):
        sl = slice(c * FF_CHUNK, (c + 1) * FF_CHUNK)
        g = _dot(h, wg_ref[:, sl])
        u = _dot(h, wu_ref[:, sl])
        a = (g * jax.nn.sigmoid(g) * u).astype(BF16)
        part = _dot(a, wo_ref[sl, :])
        if c == 0:
            acc_ref[...] = part
        else:
            acc_ref[...] += part
    o_ref[...] = x + 0.5 * acc_ref[...]


def _ffn(x2, g, wi, wo):
    n = x2.shape[0]
    tm = min(TOKEN_TILE, n)
    wg = wi[:, :D_FF].astype(BF16)
    wu = wi[:, D_FF:].astype(BF16)
    return pl.pallas_call(
        _ffn_kernel,
        out_shape=jax.ShapeDtypeStruct((n, D_MODEL), F32),
        grid=(n // tm,),
        in_specs=[
            pl.BlockSpec((tm, D_MODEL), lambda i: (i, 0)),
            _const_spec((1, D_MODEL)),
            _const_spec((D_MODEL, D_FF)),
            _const_spec((D_MODEL, D_FF)),
            _const_spec((D_FF, D_MODEL)),
        ],
        out_specs=pl.BlockSpec((tm, D_MODEL), lambda i: (i, 0)),
        scratch_shapes=[pltpu.VMEM((tm, D_MODEL), F32)],
        compiler_params=_params("parallel"),
        name="ffn",
    )(x2, g.reshape(1, D_MODEL), wg, wu, wo.astype(BF16))


def _ple_kernel(x_ref, p_ref, g_ref, wpg_ref, wpe_ref, *rest, final):
    o_ref = rest[-1]
    x = x_ref[...]
    h = _rms(x, g_ref[...]).astype(BF16)
    gate = jax.nn.sigmoid(_dot(h, wpg_ref[...]))
    y = x + gate * _dot(p_ref[...].astype(BF16), wpe_ref[...])
    if final:
        y = _rms(y, rest[0][...])
    o_ref[...] = y


def _ple(x2, p2, g, wpg, wpe, final_g=None):
    n = x2.shape[0]
    tm = min(TOKEN_TILE, n)
    final = final_g is not None
    in_specs = [
        pl.BlockSpec((tm, D_MODEL), lambda i: (i, 0)),
        pl.BlockSpec((tm, PLE_DIM), lambda i: (i, 0)),
        _const_spec((1, D_MODEL)),
        _const_spec((D_MODEL, D_MODEL)),
        _const_spec((PLE_DIM, D_MODEL)),
    ]
    args = [x2, p2, g.reshape(1, D_MODEL), wpg.astype(BF16), wpe.astype(BF16)]
    if final:
        in_specs.append(_const_spec((1, D_MODEL)))
        args.append(final_g.reshape(1, D_MODEL))
    return pl.pallas_call(
        functools.partial(_ple_kernel, final=final),
        out_shape=jax.ShapeDtypeStruct((n, D_MODEL), F32),
        grid=(n // tm,),
        in_specs=in_specs,
        out_specs=pl.BlockSpec((tm, D_MODEL), lambda i: (i, 0)),
        compiler_params=_params("parallel"),
        name="ple_final" if final else "ple",
    )(*args)


def _pre_kernel(x_ref, g_ref, w_ref, bf_ref, tril_ref, selq_ref, selk_ref, rowq_ref, rowk_ref,
                qn_ref, kvn_ref, wuqa_ref, wuqb_ref, wukvk_ref, wukvv_ref, rope_ref, lb_ref,
                fq_ref, fk_ref, fv_ref, mq_ref, mk_ref, mv_ref, sq_ref, sk_ref, sv_ref,
                hq_ref, hk_ref, hlf_ref, hv_ref, hg_ref, carry_ref):
    @pl.when(pl.program_id(1) == 0)
    def _():
        carry_ref[...] = jnp.zeros_like(carry_ref)

    h = _rms(x_ref[0], g_ref[...]).astype(BF16)

    def proj(name):
        lo, hi = PRE_COLS[name]
        return _dot(h, w_ref[:, lo:hi])

    ff_mckv = proj("ff_mckv")
    log_f = _log_sigmoid(ff_mckv[:, :LANES] + bf_ref[...])
    tril = tril_ref[...]
    c = carry_ref[...]
    for piece in _split3(log_f):
        c = c + _dot(tril, piece)
    carry_ref[...] = c[c.shape[0] - 1:, :]
    c_parts = jnp.concatenate(_split3(c * LOG2E), axis=-1)
    cq = (_dot(c_parts, selq_ref[...]) + rowq_ref[...]).astype(BF16)
    ck = (_dot(c_parts, selk_ref[...]) + rowk_ref[...]).astype(BF16)
    fq = (proj("fq") * LOG2E).astype(BF16)
    fk = proj("fk").astype(BF16)
    for p in range(2):
        pair = slice(p * LANES, (p + 1) * LANES)
        fq_ref[0, :, 2 * p * LANES:(2 * p + 1) * LANES] = fq[:, pair]
        fq_ref[0, :, (2 * p + 1) * LANES:(2 * p + 2) * LANES] = cq[:, pair]
        fk_ref[0, :, 2 * p * LANES:(2 * p + 1) * LANES] = fk[:, pair]
        fk_ref[0, :, (2 * p + 1) * LANES:(2 * p + 2) * LANES] = ck[:, pair]
    fv_ref[0] = proj("fv").astype(BF16)

    rope = rope_ref[...]
    cos_q, sin_q = rope[:, 0:LANES], rope[:, LANES:2 * LANES]
    cos_k, sin_k = rope[:, 2 * LANES:3 * LANES], rope[:, 3 * LANES:4 * LANES]
    cq_lat = _rms(proj("mcq"), qn_ref[...]).astype(BF16)
    qa = _dot(cq_lat, wuqa_ref[...])
    qb = _dot(cq_lat, wuqb_ref[...])
    ckv_lat = _rms(ff_mckv[:, LANES:], kvn_ref[...]).astype(BF16)
    k_nope = _dot(ckv_lat, wukvk_ref[...])
    mkr_pair = proj("mkr_pair")
    k_rope = mkr_pair[:, :LANES] * cos_k + mkr_pair[:, LANES:] * sin_k
    for hd in range(N_HEADS):
        sl = slice(hd * LANES, (hd + 1) * LANES)
        mq_ref[0, :, sl] = (qa[:, sl] * cos_q + qb[:, sl] * sin_q).astype(BF16)
        mk_ref[0, :, sl] = (k_nope[:, sl] + k_rope).astype(BF16)
    mv_ref[0] = _dot(ckv_lat, wukvv_ref[...]).astype(BF16)

    sq_ref[0] = (proj("sq") * LOG2E).astype(BF16)
    sk_ref[0] = proj("sk").astype(BF16)
    sv_ref[0] = proj("sv").astype(BF16)

    lb = lb_ref[...]
    f = lb + (1.0 - lb) * jax.nn.sigmoid(proj("hf"))
    hq_ref[0] = proj("hq").astype(BF16)
    hk_ref[0] = 1.0 - f
    hlf_ref[0] = jnp.log(f) * LOG2E
    hv_ref[0] = proj("hi").astype(BF16)
    hg = proj("hg")
    hg_ref[0] = (hg * jax.nn.sigmoid(hg)).astype(BF16)


def _pre_weights(w_in, b_f, w_uq, w_ukv):
    cols = lambda j: w_in[:, IN_OFFS[j]:IN_OFFS[j + 1]]
    zeros = lambda n: jnp.zeros((D_MODEL, n), F32)
    half = MLA_ROPE // 2
    mkr = cols(6)
    mkr_sw = jnp.concatenate([-mkr[:, half:], mkr[:, :half]], axis=1)
    pad_rope = lambda t: jnp.concatenate([zeros(MLA_NOPE), t, zeros(LANES - MLA_NOPE - MLA_ROPE)], axis=1)
    scale = HEAD_DIM ** -0.5
    w = jnp.concatenate([
        cols(0) * scale, cols(1), cols(2),
        jnp.concatenate([cols(3), zeros(LANES - N_HEADS)], axis=1),
        cols(5), cols(4), pad_rope(mkr), pad_rope(mkr_sw),
        cols(7) * scale, cols(8), cols(9),
        cols(10), cols(11), cols(12), cols(13)], axis=1).astype(BF16)
    bf = jnp.concatenate([b_f, jnp.zeros((LANES - N_HEADS,), F32)]).reshape(1, LANES)

    dq = MLA_NOPE + MLA_ROPE
    uq = w_uq.reshape(MLA_Q_RANK, N_HEADS, dq)
    zq = jnp.zeros((MLA_Q_RANK, N_HEADS, LANES - dq), F32)
    uqa = jnp.concatenate([uq, zq], axis=2).reshape(MLA_Q_RANK, N_HEADS * LANES)
    rope_cols = uq[:, :, MLA_NOPE:]
    rope_sw = jnp.concatenate([-rope_cols[:, :, half:], rope_cols[:, :, :half]], axis=2)
    uqb = jnp.concatenate([jnp.zeros((MLA_Q_RANK, N_HEADS, MLA_NOPE), F32), rope_sw, zq],
                          axis=2).reshape(MLA_Q_RANK, N_HEADS * LANES)
    ukv = w_ukv.reshape(MLA_KV_RANK, N_HEADS, MLA_NOPE + HEAD_DIM)
    ukvk = jnp.concatenate([ukv[:, :, :MLA_NOPE], jnp.zeros((MLA_KV_RANK, N_HEADS, LANES - MLA_NOPE), F32)],
                           axis=2).reshape(MLA_KV_RANK, N_HEADS * LANES)
    ukvv = ukv[:, :, MLA_NOPE:].reshape(MLA_KV_RANK, N_HEADS * HEAD_DIM)
    return w, bf, uqa.astype(BF16), uqb.astype(BF16), ukvk.astype(BF16), ukvv.astype(BF16)


def _fox_selectors():
    selq = np.zeros((3 * LANES, 2 * LANES), np.float32)
    selk = np.zeros((3 * LANES, 2 * LANES), np.float32)
    rowq = np.zeros((1, 2 * LANES), np.float32)
    rowk = np.zeros((1, 2 * LANES), np.float32)
    for hd in range(N_HEADS):
        base = (hd // 2) * LANES + (hd % 2) * 8
        for piece in range(3):
            selq[piece * LANES + hd, base + piece] = 1.0
            selk[piece * LANES + hd, base + 3 + piece] = -1.0
            rowq[0, base + 3 + piece] = 1.0
            rowk[0, base + piece] = 1.0
    return (jnp.asarray(selq, BF16), jnp.asarray(selk, BF16), jnp.asarray(rowq), jnp.asarray(rowk))


def _rope_tables(s_len):
    half = MLA_ROPE // 2
    inv = ROPE_BASE ** (-jnp.arange(half, dtype=F32) / half)
    ang = jnp.arange(s_len).astype(F32)[:, None] * inv[None, :]
    cos, sin = jnp.cos(ang), jnp.sin(ang)
    tail = jnp.zeros((s_len, LANES - MLA_NOPE - MLA_ROPE), F32)
    cos_t = jnp.concatenate([jnp.ones((s_len, MLA_NOPE), F32), cos, cos, tail], axis=1)
    sin_t = jnp.concatenate([jnp.zeros((s_len, MLA_NOPE), F32), sin, sin, tail], axis=1)
    scale = (MLA_NOPE + MLA_ROPE) ** -0.5 * LOG2E
    return jnp.concatenate([cos_t * scale, sin_t * scale, cos_t, sin_t], axis=1)


def _pre(x3, g, w, bf, uqa, uqb, ukvk, ukvv, qn, kvn, lb, consts):
    b, s_len, _ = x3.shape
    tm = min(TOKEN_TILE, s_len)
    tril, selq, selk, rowq, rowk, rope = consts
    tok = lambda width: pl.BlockSpec((1, tm, width), lambda bi, si: (bi, si, 0))
    bf16_out = lambda width: jax.ShapeDtypeStruct((b, s_len, width), BF16)
    f32_out = lambda width: jax.ShapeDtypeStruct((b, s_len, width), F32)
    in_specs = [
        tok(D_MODEL), _const_spec((1, D_MODEL)), _const_spec((D_MODEL, PRE_W)), _const_spec((1, LANES)),
        _const_spec((tm, tm)), _const_spec(selq.shape), _const_spec(selk.shape),
        _const_spec(rowq.shape), _const_spec(rowk.shape),
        _const_spec((1, MLA_Q_RANK)), _const_spec((1, MLA_KV_RANK)),
        _const_spec(uqa.shape), _const_spec(uqb.shape), _const_spec(ukvk.shape), _const_spec(ukvv.shape),
        pl.BlockSpec((tm, 4 * LANES), lambda bi, si: (si, 0)),
        _const_spec((1, BRANCH_W)),
    ]
    widths = [(512, BF16), (512, BF16), (256, BF16), (512, BF16), (512, BF16), (256, BF16),
              (256, BF16), (256, BF16), (256, BF16),
              (256, BF16), (256, F32), (256, F32), (256, BF16), (256, BF16)]
    return pl.pallas_call(
        _pre_kernel,
        out_shape=[bf16_out(wd) if dt == BF16 else f32_out(wd) for wd, dt in widths],
        grid=(b, s_len // tm),
        in_specs=in_specs,
        out_specs=[tok(wd) for wd, _ in widths],
        scratch_shapes=[pltpu.VMEM((1, LANES), F32)],
        compiler_params=_params("arbitrary", "arbitrary"),
        name="pre_mix",
    )(x3, g.reshape(1, D_MODEL), w, bf, tril, selq, selk, rowq, rowk,
      qn.reshape(1, -1), kvn.reshape(1, -1), uqa, uqb, ukvk, ukvv, rope, lb.reshape(1, -1))


def _lane_mask(shape, ranges):
    lane = lax.broadcasted_iota(jnp.int32, shape, 1)
    m = None
    for lo, hi in ranges:
        r = (lane >= lo) & (lane < hi)
        m = r if m is None else (m | r)
    return m


def _softmax_attn_kernel(q_ref, k_ref, v_ref, o_ref, qm_ref, s_ref, m_ref, l_ref, acc_ref, *, t, lanes_a, lanes_b):
    qi = pl.program_id(2)
    q = q_ref[0]
    zero = jnp.zeros_like(q)
    qm_ref[0] = jnp.where(_lane_mask(q.shape, lanes_a), q, zero)
    qm_ref[1] = jnp.where(_lane_mask(q.shape, lanes_b), q, zero)
    m_ref[...] = jnp.full(m_ref.shape, -1e30, F32)
    l_ref[...] = jnp.zeros(l_ref.shape, F32)
    acc_ref[...] = jnp.zeros(acc_ref.shape, F32)

    def logits(j, slot):
        kb = k_ref[0, pl.ds(pl.multiple_of(j * t, t), t), :]
        for hd in range(2):
            s_ref[slot, hd] = _dot_nt(qm_ref[hd], kb)

    def update(j, slot, diagonal):
        vb = v_ref[0, pl.ds(pl.multiple_of(j * t, t), t), :]
        for hd in range(2):
            s = s_ref[slot, hd]
            if diagonal:
                row = lax.broadcasted_iota(jnp.int32, (t, t), 0)
                col = lax.broadcasted_iota(jnp.int32, (t, t), 1)
                s = jnp.where(col <= row, s, -1e30)
            m = m_ref[hd]
            m_new = jnp.maximum(m, jnp.max(s, axis=-1, keepdims=True))
            alpha = jnp.exp2(m - m_new)
            p = jnp.exp2(s - jnp.concatenate([m_new] * (t // LANES), axis=1))
            m_ref[hd] = m_new
            l_ref[hd] = alpha * l_ref[hd] + jnp.sum(p, axis=-1, keepdims=True)
            acc_ref[hd] = alpha * acc_ref[hd] + _dot(p.astype(BF16), vb)

    logits(0, 0)

    def body(j, carry):
        slot = lax.rem(j, 2)
        update(j, slot, False)
        logits(j + 1, 1 - slot)
        return carry

    lax.fori_loop(0, qi, body, 0)
    update(qi, lax.rem(qi, 2), True)
    first_half = _lane_mask((t, LANES), ((0, HEAD_DIM),))
    o_ref[0] = jnp.where(first_half, acc_ref[0] / l_ref[0], acc_ref[1] / l_ref[1]).astype(BF16)


def _softmax_attn(q, k, v, kw, lanes_a, lanes_b, name):
    b, s_len, _ = q.shape
    t = min(ATTN_TILE, s_len)
    return pl.pallas_call(
        functools.partial(_softmax_attn_kernel, t=t, lanes_a=lanes_a, lanes_b=lanes_b),
        out_shape=jax.ShapeDtypeStruct((b, s_len, BRANCH_W), BF16),
        grid=(b, 2, s_len // t),
        in_specs=[
            pl.BlockSpec((1, t, kw), lambda bi, hp, qi: (bi, qi, hp)),
            pl.BlockSpec((1, s_len, kw), lambda bi, hp, qi: (bi, 0, hp)),
            pl.BlockSpec((1, s_len, LANES), lambda bi, hp, qi: (bi, 0, hp)),
        ],
        out_specs=pl.BlockSpec((1, t, LANES), lambda bi, hp, qi: (bi, qi, hp)),
        scratch_shapes=[pltpu.VMEM((2, t, kw), BF16), pltpu.VMEM((2, 2, t, t), F32),
                        pltpu.VMEM((2, t, LANES), F32), pltpu.VMEM((2, t, LANES), F32),
                        pltpu.VMEM((2, t, LANES), F32)],
        compiler_params=_params("parallel", "parallel", "arbitrary"),
        name=name,
    )(q, k, v)


def _sb_attn_kernel(q_ref, k_ref, v_ref, u_ref, o_ref, qm_ref, rest_ref, acc_ref, *, tk, nsub):
    tq = nsub * tk
    qi = pl.program_id(2)
    q = q_ref[0]
    zero = jnp.zeros_like(q)
    qm_ref[0] = jnp.where(_lane_mask(q.shape, ((0, HEAD_DIM),)), q, zero)
    qm_ref[1] = jnp.where(_lane_mask(q.shape, ((HEAD_DIM, 2 * HEAD_DIM),)), q, zero)
    rest_ref[...] = jnp.zeros(rest_ref.shape, F32)
    acc_ref[...] = jnp.zeros(acc_ref.shape, F32)

    def weights(hd, r0, nrows, blk, diag):
        rows = pl.ds(r0, nrows)
        off = pl.multiple_of(blk * tk, tk)
        z = _dot_nt(qm_ref[hd, rows, :], k_ref[0, pl.ds(off, tk), :])
        neg_abs = pltpu.bitcast(pltpu.bitcast(z, jnp.uint32) | jnp.uint32(0x80000000), F32)
        log_beta = jnp.minimum(z, 0.0) - jnp.log2(1.0 + jnp.exp2(neg_abs))
        log_rest = log_beta - z
        if diag:
            row = lax.broadcasted_iota(jnp.int32, (tk, tk), 0)
            col = lax.broadcasted_iota(jnp.int32, (tk, tk), 1)
            strict = col < row
            log_rest = jnp.where(strict, log_rest, 0.0)
        rest = rest_ref[hd, rows, :]
        later = _dot(log_rest.astype(BF16), u_ref[...]) + jnp.concatenate([rest] * (tk // LANES), axis=1)
        a = jnp.exp2(log_beta + later)
        if diag:
            a = jnp.where(strict, a, 0.0)
        rest_ref[hd, rows, :] = rest + jnp.sum(log_rest, axis=-1, keepdims=True)
        return a.astype(BF16)

    def block(hd, r0, nrows, blk, diag):
        a = weights(hd, r0, nrows, blk, diag)
        acc_ref[hd, pl.ds(r0, nrows), :] += _dot(a, v_ref[0, pl.ds(pl.multiple_of(blk * tk, tk), tk), :])

    for hd in range(2):
        for d in reversed(range(nsub)):
            block(hd, d * tk, tk, nsub * qi + d, True)
            if d < nsub - 1:
                block(hd, (d + 1) * tk, tq - (d + 1) * tk, nsub * qi + d, False)

    def body(j, carry):
        chunk = qi - 1 - j
        off = pl.multiple_of(chunk * tq, tq)
        for hd in range(2):
            parts = [None] * nsub
            for d in reversed(range(nsub)):
                parts[d] = weights(hd, 0, tq, nsub * chunk + d, False)
            acc_ref[hd] += _dot(jnp.concatenate(parts, axis=1), v_ref[0, pl.ds(off, tq), :])
        return carry

    lax.fori_loop(0, qi, body, 0)
    first_half = _lane_mask((tq, LANES), ((0, HEAD_DIM),))
    o_ref[0] = jnp.where(first_half, acc_ref[0], acc_ref[1]).astype(BF16)


def _sb_attn(q, k, v, u):
    b, s_len, _ = q.shape
    tk = u.shape[0]
    nsub = min(SB_SUBBLOCKS, s_len // tk)
    tq = nsub * tk
    return pl.pallas_call(
        functools.partial(_sb_attn_kernel, tk=tk, nsub=nsub),
        out_shape=jax.ShapeDtypeStruct((b, s_len, BRANCH_W), BF16),
        grid=(b, 2, s_len // tq),
        in_specs=[
            pl.BlockSpec((1, tq, LANES), lambda bi, hp, qi: (bi, qi, hp)),
            pl.BlockSpec((1, s_len, LANES), lambda bi, hp, qi: (bi, 0, hp)),
            pl.BlockSpec((1, s_len, LANES), lambda bi, hp, qi: (bi, 0, hp)),
            _const_spec((tk, tk)),
        ],
        out_specs=pl.BlockSpec((1, tq, LANES), lambda bi, hp, qi: (bi, qi, hp)),
        scratch_shapes=[pltpu.VMEM((2, tq, LANES), BF16), pltpu.VMEM((2, tq, LANES), F32),
                        pltpu.VMEM((2, tq, LANES), F32)],
        compiler_params=_params("parallel", "parallel", "arbitrary"),
        name="stick_breaking",
    )(q, k, v, u)


def _hgrn_kernel(q_ref, k_ref, lf_ref, v_ref, gate_ref, on_ref, btril_ref, bones_ref, o_ref, st_ref, *, ts):
    @pl.when(pl.program_id(1) == 0)
    def _():
        st_ref[...] = jnp.zeros_like(st_ref)

    c = HGRN_CHUNK
    w = BRANCH_W
    btril = btril_ref[...]
    bones = bones_ref[...]
    head_mask = bones.astype(F32)
    cum = None
    for piece in _split3(lf_ref[0]):
        part = _dot(btril, piece)
        cum = part if cum is None else cum + part

    sub = 8
    groups = [(g * sub, (g + 1) * sub) for g in range(c // sub)]
    causal = [(t0 + lax.broadcasted_iota(jnp.int32, (sub, ns, w), 0)) >= lax.broadcasted_iota(jnp.int32, (sub, ns, w), 1)
              for t0, ns in groups]
    outs = []
    for ci in range(ts // c):
        r = slice(ci * c, (ci + 1) * c)
        bc = cum[r]
        qc = q_ref[0, r, :].astype(F32)
        kc = k_ref[0, r, :]
        vc = v_ref[0, r, :].astype(F32)
        prods = []
        for (t0, ns), mask in zip(groups, causal):
            bt = bc[t0:t0 + sub]
            dec = jnp.exp2(jnp.where(mask, bt[:, None, :] - bc[None, :ns, :], -jnp.inf))
            prods.append((qc[t0:t0 + sub, None, :] * dec * kc[None, :ns, :]).reshape(sub * ns, w).astype(BF16))
        scores = _dot(jnp.concatenate(prods, axis=0), bones)
        o_parts = []
        row0 = 0
        for t0, ns in groups:
            sc = scores[row0:row0 + sub * ns].reshape(sub, ns, w)
            o_parts.append(jnp.sum(sc * vc[None, :ns, :], axis=1))
            row0 += sub * ns
        o_intra = jnp.concatenate(o_parts, axis=0)
        st = st_ref[...]
        o_inter = _dot_nt((qc * jnp.exp2(bc)).astype(BF16), st.astype(BF16))
        b_last = bc[c - 1:c, :]
        k_dec = (kc * jnp.exp2(b_last - bc)).astype(BF16)
        upd = _dot(vc.T.astype(BF16), k_dec)
        st_ref[...] = st * jnp.exp2(b_last) + upd * head_mask
        outs.append(o_intra + o_inter)
    o = jnp.concatenate(outs, axis=0)
    sq_hi = (o * o).astype(BF16)
    sq_lo = (o * o - sq_hi.astype(F32)).astype(BF16)
    ms = (_dot(sq_hi, bones) + _dot(sq_lo, bones)) * (1.0 / HEAD_DIM)
    y = o * lax.rsqrt(ms + EPS) * on_ref[...] * gate_ref[0].astype(F32)
    o_ref[0] = y.astype(BF16)


def _hgrn(q, k, lf, v, gate, out_norm, btril, bones):
    b, s_len, w = q.shape
    ts = btril.shape[0]
    tok = pl.BlockSpec((1, ts, w), lambda bi, si: (bi, si, 0))
    return pl.pallas_call(
        functools.partial(_hgrn_kernel, ts=ts),
        out_shape=jax.ShapeDtypeStruct((b, s_len, w), BF16),
        grid=(b, s_len // ts),
        in_specs=[tok, tok, tok, tok, tok, _const_spec((1, w)), _const_spec((ts, ts)), _const_spec((w, w))],
        out_specs=tok,
        scratch_shapes=[pltpu.VMEM((w, w), F32)],
        compiler_params=_params("arbitrary", "arbitrary"),
        name="hgrn2",
    )(q, k, lf, v, gate, out_norm.reshape(1, w), btril, bones)


def _merge_kernel(x_ref, ya_ref, yb_ref, yc_ref, yd_ref, g_ref, wgate_ref, wbr_ref, wout_ref, o_ref):
    x = x_ref[...]
    h = _rms(x, g_ref[...]).astype(BF16)
    mixed = None
    for m, y_ref in enumerate((ya_ref, yb_ref, yc_ref, yd_ref)):
        gate = jax.nn.sigmoid(_dot(h, wgate_ref[:, m * D_MODEL:(m + 1) * D_MODEL]))
        term = gate * _dot(y_ref[...], wbr_ref[m])
        mixed = term if mixed is None else mixed + term
    o_ref[...] = x + _dot(mixed.astype(BF16), wout_ref[...])


def _merge(x2, ys, g, wgate, wbr, wout):
    n = x2.shape[0]
    tm = min(TOKEN_TILE, n)
    ytok = pl.BlockSpec((tm, BRANCH_W), lambda i: (i, 0))
    return pl.pallas_call(
        _merge_kernel,
        out_shape=jax.ShapeDtypeStruct((n, D_MODEL), F32),
        grid=(n // tm,),
        in_specs=[pl.BlockSpec((tm, D_MODEL), lambda i: (i, 0)), ytok, ytok, ytok, ytok,
                  _const_spec((1, D_MODEL)), _const_spec((D_MODEL, N_BRANCH * D_MODEL)),
                  _const_spec((N_BRANCH, BRANCH_W, D_MODEL)), _const_spec((D_MODEL, D_MODEL))],
        out_specs=pl.BlockSpec((tm, D_MODEL), lambda i: (i, 0)),
        compiler_params=_params("parallel"),
        name="merge",
    )(x2, *ys, g.reshape(1, D_MODEL), wgate, wbr, wout)


def kernel(x, p, w_in, b_fox_f, mla_q_norm, w_mla_uq, mla_kv_norm, w_mla_ukv, hgrn_lb_logits, hgrn_out_norm,
           w_branch, w_out, ffn_a_wi, ffn_a_wo, ffn_b_wi, ffn_b_wo, w_ple_in, w_ple_gate, norms, final_norm):
    b, s_len, _ = x.shape
    n = b * s_len
    depth = w_in.shape[0]
    tm = min(TOKEN_TILE, s_len)
    t = min(SB_KEY_TILE, s_len)
    ts = min(HGRN_TILE, s_len)

    tril = jnp.tril(jnp.ones((tm, tm), F32)).astype(BF16)
    selq, selk, rowq, rowk = _fox_selectors()
    rope = _rope_tables(s_len)
    suffix = jnp.tril(jnp.ones((t, t), F32), -1).astype(BF16)
    chunk_id = np.arange(ts) // HGRN_CHUNK
    btril = jnp.asarray((chunk_id[:, None] == chunk_id[None, :]) & (np.arange(ts)[:, None] >= np.arange(ts)[None, :]),
                        BF16)
    head_id = np.arange(BRANCH_W) // HEAD_DIM
    bones = jnp.asarray(head_id[:, None] == head_id[None, :], BF16)

    lb_cum = jnp.cumsum(jax.nn.softmax(hgrn_lb_logits.astype(F32), axis=0), axis=0)
    lower_bounds = lb_cum - lb_cum[0]

    fox_a = ((0, HEAD_DIM), (LANES, LANES + 8))
    fox_b = ((HEAD_DIM, LANES), (LANES + 8, LANES + 16))
    mla_a = ((0, LANES),)
    mla_b = ((LANES, 2 * LANES),)

    x2 = x.reshape(n, D_MODEL)
    for i in range(depth):
        x2 = _ffn(x2, norms[i, 0], ffn_a_wi[i], ffn_a_wo[i])
        w, bf, uqa, uqb, ukvk, ukvv = _pre_weights(w_in[i], b_fox_f[i], w_mla_uq[i], w_mla_ukv[i])
        (fq, fk, fv, mq, mk, mv, sq, sk, sv, hq, hk, hlf, hv, hg) = _pre(
            x2.reshape(b, s_len, D_MODEL), norms[i, 1], w, bf, uqa, uqb, ukvk, ukvv,
            mla_q_norm[i], mla_kv_norm[i], lower_bounds[i], (tril, selq, selk, rowq, rowk, rope))
        y_a = _softmax_attn(fq, fk, fv, 2 * LANES, fox_a, fox_b, "fox")
        y_b = _softmax_attn(mq, mk, mv, 2 * LANES, mla_a, mla_b, "mla")
        y_c = _sb_attn(sq, sk, sv, suffix)
        y_d = _hgrn(hq, hk, hlf, hv, hg, hgrn_out_norm[i], btril, bones)
        ys = [y.reshape(n, BRANCH_W) for y in (y_a, y_b, y_c, y_d)]
        x2 = _merge(x2, ys, norms[i, 1], w_in[i][:, IN_OFFS[14]:].astype(BF16),
                    w_branch[i].astype(BF16), w_out[i].astype(BF16))
        x2 = _ffn(x2, norms[i, 2], ffn_b_wi[i], ffn_b_wo[i])
        x2 = _ple(x2, p[i].reshape(n, PLE_DIM), norms[i, 3], w_ple_gate[i], w_ple_in[i],
                  final_norm if i == depth - 1 else None)
    return x2.reshape(b, s_len, D_MODEL)
```

```python
import functools

import jax
import jax.numpy as jnp
import numpy as np
from jax import lax
from jax.experimental import pallas as pl
from jax.experimental.pallas import tpu as pltpu

F32 = jnp.float32
BF16 = jnp.bfloat16

D_MODEL = 1024
PLE_DIM = 256
D_FF = 2816
EPS = 1e-6
LOG2E = 1.4426950408889634
N_BRANCH = 4
N_HEADS = 4
HEAD_DIM = 64
BRANCH_W = 256
MLA_NOPE = 64
MLA_ROPE = 32
MLA_Q_RANK = 256
MLA_KV_RANK = 128
ROPE_BASE = 10000.0
HGRN_CHUNK = 32
LANES = 128

IN_SPLITS = (256, 256, 256, 4, 256, 128, 32, 256, 256, 256, 256, 256, 256, 256, 4096)
IN_OFFS = tuple(int(o) for o in np.cumsum((0,) + IN_SPLITS))

PRE_COLS = dict(
    fq=(0, 256), fk=(256, 512), fv=(512, 768), ff_mckv=(768, 1024),
    mcq=(1024, 1280), mkr_pair=(1280, 1536),
    sq=(1536, 1792), sk=(1792, 2048), sv=(2048, 2304),
    hq=(2304, 2560), hf=(2560, 2816), hi=(2816, 3072), hg=(3072, 3328),
)
PRE_W = 3328

FF_CHUNK = 256
TOKEN_TILE = 512
ATTN_TILE = 1024
SB_KEY_TILE = 256
SB_SUBBLOCKS = 4
HGRN_TILE = 256
VMEM_LIMIT = 56 * 1024 * 1024


def _dot(a, b):
    return jnp.dot(a, b, preferred_element_type=F32)


def _dot_nt(a, b):
    return lax.dot_general(a, b, (((1,), (1,)), ((), ())), preferred_element_type=F32)


def _rms(x, g):
    return x * lax.rsqrt(jnp.mean(x * x, axis=-1, keepdims=True) + EPS) * g


def _log_sigmoid(x):
    return jnp.minimum(x, 0.0) - jnp.log1p(jnp.exp(-jnp.abs(x)))


def _split3(x):
    hi = x.astype(BF16)
    r1 = x - hi.astype(F32)
    mid = r1.astype(BF16)
    lo = (r1 - mid.astype(F32)).astype(BF16)
    return hi, mid, lo


def _params(*sem):
    return pltpu.CompilerParams(dimension_semantics=sem, vmem_limit_bytes=VMEM_LIMIT)


def _const_spec(shape):
    nd = len(shape)
    return pl.BlockSpec(shape, lambda *_: (0,) * nd, pipeline_mode=pl.Buffered(1))


def _ffn_kernel(x_ref, g_ref, wg_ref, wu_ref, wo_ref, o_ref, acc_ref):
    x = x_ref[...]
    h = _rms(x, g_ref[...]).astype(BF16)
    for c in range(D_FF // FF_CHUNK):
        sl = slice(c * FF_CHUNK, (c + 1) * FF_CHUNK)
        g = _dot(h, wg_ref[:, sl])
        u = _dot(h, wu_ref[:, sl])
        a = (g * jax.nn.sigmoid(g) * u).astype(BF16)
        part = _dot(a, wo_ref[sl, :])
        if c == 0:
            acc_ref[...] = part
        else:
            acc_ref[...] += part
    o_ref[...] = x + 0.5 * acc_ref[...]


def _ffn(x2, g, wi, wo):
    n = x2.shape[0]
    tm = min(TOKEN_TILE, n)
    wg = wi[:, :D_FF].astype(BF16)
    wu = wi[:, D_FF:].astype(BF16)
    return pl.pallas_call(
        _ffn_kernel,
        out_shape=jax.ShapeDtypeStruct((n, D_MODEL), F32),
        grid=(n // tm,),
        in_specs=[
            pl.BlockSpec((tm, D_MODEL), lambda i: (i, 0)),
            _const_spec((1, D_MODEL)),
            _const_spec((D_MODEL, D_FF)),
            _const_spec((D_MODEL, D_FF)),
            _const_spec((D_FF, D_MODEL)),
        ],
        out_specs=pl.BlockSpec((tm, D_MODEL), lambda i: (i, 0)),
        scratch_shapes=[pltpu.VMEM((tm, D_MODEL), F32)],
        compiler_params=_params("parallel"),
        name="ffn",
    )(x2, g.reshape(1, D_MODEL), wg, wu, wo.astype(BF16))


def _ple_kernel(x_ref, p_ref, g_ref, wpg_ref, wpe_ref, *rest, final):
    o_ref = rest[-1]
    x = x_ref[...]
    h = _rms(x, g_ref[...]).astype(BF16)
    gate = jax.nn.sigmoid(_dot(h, wpg_ref[...]))
    y = x + gate * _dot(p_ref[...].astype(BF16), wpe_ref[...])
    if final:
        y = _rms(y, rest[0][...])
    o_ref[...] = y


def _ple(x2, p2, g, wpg, wpe, final_g=None):
    n = x2.shape[0]
    tm = min(TOKEN_TILE, n)
    final = final_g is not None
    in_specs = [
        pl.BlockSpec((tm, D_MODEL), lambda i: (i, 0)),
        pl.BlockSpec((tm, PLE_DIM), lambda i: (i, 0)),
        _const_spec((1, D_MODEL)),
        _const_spec((D_MODEL, D_MODEL)),
        _const_spec((PLE_DIM, D_MODEL)),
    ]
    args = [x2, p2, g.reshape(1, D_MODEL), wpg.astype(BF16), wpe.astype(BF16)]
    if final:
        in_specs.append(_const_spec((1, D_MODEL)))
        args.append(final_g.reshape(1, D_MODEL))
    return pl.pallas_call(
        functools.partial(_ple_kernel, final=final),
        out_shape=jax.ShapeDtypeStruct((n, D_MODEL), F32),
        grid=(n // tm,),
        in_specs=in_specs,
        out_specs=pl.BlockSpec((tm, D_MODEL), lambda i: (i, 0)),
        compiler_params=_params("parallel"),
        name="ple_final" if final else "ple",
    )(*args)


def _pre_kernel(x_ref, g_ref, w_ref, bf_ref, tril_ref, selq_ref, selk_ref, rowq_ref, rowk_ref,
                qn_ref, kvn_ref, wuqa_ref, wuqb_ref, wukvk_ref, wukvv_ref, rope_ref, lb_ref,
                fq_ref, fk_ref, fv_ref, mq_ref, mk_ref, mv_ref, sq_ref, sk_ref, sv_ref,
                hq_ref, hk_ref, hlf_ref, hv_ref, hg_ref, carry_ref):
    @pl.when(pl.program_id(1) == 0)
    def _():
        carry_ref[...] = jnp.zeros_like(carry_ref)

    h = _rms(x_ref[0], g_ref[...]).astype(BF16)

    def proj(name):
        lo, hi = PRE_COLS[name]
        return _dot(h, w_ref[:, lo:hi])

    ff_mckv = proj("ff_mckv")
    log_f = _log_sigmoid(ff_mckv[:, :LANES] + bf_ref[...])
    tril = tril_ref[...]
    c = carry_ref[...]
    for piece in _split3(log_f):
        c = c + _dot(tril, piece)
    carry_ref[...] = c[c.shape[0] - 1:, :]
    c_parts = jnp.concatenate(_split3(c * LOG2E), axis=-1)
    cq = (_dot(c_parts, selq_ref[...]) + rowq_ref[...]).astype(BF16)
    ck = (_dot(c_parts, selk_ref[...]) + rowk_ref[...]).astype(BF16)
    fq = (proj("fq") * LOG2E).astype(BF16)
    fk = proj("fk").astype(BF16)
    for p in range(2):
        pair = slice(p * LANES, (p + 1) * LANES)
        fq_ref[0, :, 2 * p * LANES:(2 * p + 1) * LANES] = fq[:, pair]
        fq_ref[0, :, (2 * p + 1) * LANES:(2 * p + 2) * LANES] = cq[:, pair]
        fk_ref[0, :, 2 * p * LANES:(2 * p + 1) * LANES] = fk[:, pair]
        fk_ref[0, :, (2 * p + 1) * LANES:(2 * p + 2) * LANES] = ck[:, pair]
    fv_ref[0] = proj("fv").astype(BF16)

    rope = rope_ref[...]
    cos_q, sin_q = rope[:, 0:LANES], rope[:, LANES:2 * LANES]
    cos_k, sin_k = rope[:, 2 * LANES:3 * LANES], rope[:, 3 * LANES:4 * LANES]
    cq_lat = _rms(proj("mcq"), qn_ref[...]).astype(BF16)
    qa = _dot(cq_lat, wuqa_ref[...])
    qb = _dot(cq_lat, wuqb_ref[...])
    ckv_lat = _rms(ff_mckv[:, LANES:], kvn_ref[...]).astype(BF16)
    k_nope = _dot(ckv_lat, wukvk_ref[...])
    mkr_pair = proj("mkr_pair")
    k_rope = mkr_pair[:, :LANES] * cos_k + mkr_pair[:, LANES:] * sin_k
    for hd in range(N_HEADS):
        sl = slice(hd * LANES, (hd + 1) * LANES)
        mq_ref[0, :, sl] = (qa[:, sl] * cos_q + qb[:, sl] * sin_q).astype(BF16)
        mk_ref[0, :, sl] = (k_nope[:, sl] + k_rope).astype(BF16)
    mv_ref[0] = _dot(ckv_lat, wukvv_ref[...]).astype(BF16)

    sq_ref[0] = (proj("sq") * LOG2E).astype(BF16)
    sk_ref[0] = proj("sk").astype(BF16)
    sv_ref[0] = proj("sv").astype(BF16)

    lb = lb_ref[...]
    f = lb + (1.0 - lb) * jax.nn.sigmoid(proj("hf"))
    hq_ref[0] = proj("hq").astype(BF16)
    hk_ref[0] = 1.0 - f
    hlf_ref[0] = jnp.log(f) * LOG2E
    hv_ref[0] = proj("hi").astype(BF16)
    hg = proj("hg")
    hg_ref[0] = (hg * jax.nn.sigmoid(hg)).astype(BF16)


def _pre_weights(w_in, b_f, w_uq, w_ukv):
    cols = lambda j: w_in[:, IN_OFFS[j]:IN_OFFS[j + 1]]
    zeros = lambda n: jnp.zeros((D_MODEL, n), F32)
    half = MLA_ROPE // 2
    mkr = cols(6)
    mkr_sw = jnp.concatenate([-mkr[:, half:], mkr[:, :half]], axis=1)
    pad_rope = lambda t: jnp.concatenate([zeros(MLA_NOPE), t, zeros(LANES - MLA_NOPE - MLA_ROPE)], axis=1)
    scale = HEAD_DIM ** -0.5
    w = jnp.concatenate([
        cols(0) * scale, cols(1), cols(2),
        jnp.concatenate([cols(3), zeros(LANES - N_HEADS)], axis=1),
        cols(5), cols(4), pad_rope(mkr), pad_rope(mkr_sw),
        cols(7) * scale, cols(8), cols(9),
        cols(10), cols(11), cols(12), cols(13)], axis=1).astype(BF16)
    bf = jnp.concatenate([b_f, jnp.zeros((LANES - N_HEADS,), F32)]).reshape(1, LANES)

    dq = MLA_NOPE + MLA_ROPE
    uq = w_uq.reshape(MLA_Q_RANK, N_HEADS, dq)
    zq = jnp.zeros((MLA_Q_RANK, N_HEADS, LANES - dq), F32)
    uqa = jnp.concatenate([uq, zq], axis=2).reshape(MLA_Q_RANK, N_HEADS * LANES)
    rope_cols = uq[:, :, MLA_NOPE:]
    rope_sw = jnp.concatenate([-rope_cols[:, :, half:], rope_cols[:, :, :half]], axis=2)
    uqb = jnp.concatenate([jnp.zeros((MLA_Q_RANK, N_HEADS, MLA_NOPE), F32), rope_sw, zq],
                          axis=2).reshape(MLA_Q_RANK, N_HEADS * LANES)
    ukv = w_ukv.reshape(MLA_KV_RANK, N_HEADS, MLA_NOPE + HEAD_DIM)
    ukvk = jnp.concatenate([ukv[:, :, :MLA_NOPE], jnp.zeros((MLA_KV_RANK, N_HEADS, LANES - MLA_NOPE), F32)],
                           axis=2).reshape(MLA_KV_RANK, N_HEADS * LANES)
    ukvv = ukv[:, :, MLA_NOPE:].reshape(MLA_KV_RANK, N_HEADS * HEAD_DIM)
    return w, bf, uqa.astype(BF16), uqb.astype(BF16), ukvk.astype(BF16), ukvv.astype(BF16)


def _fox_selectors():
    selq = np.zeros((3 * LANES, 2 * LANES), np.float32)
    selk = np.zeros((3 * LANES, 2 * LANES), np.float32)
    rowq = np.zeros((1, 2 * LANES), np.float32)
    rowk = np.zeros((1, 2 * LANES), np.float32)
    for hd in range(N_HEADS):
        base = (hd // 2) * LANES + (hd % 2) * 8
        for piece in range(3):
            selq[piece * LANES + hd, base + piece] = 1.0
            selk[piece * LANES + hd, base + 3 + piece] = -1.0
            rowq[0, base + 3 + piece] = 1.0
            rowk[0, base + piece] = 1.0
    return (jnp.asarray(selq, BF16), jnp.asarray(selk, BF16), jnp.asarray(rowq), jnp.asarray(rowk))


def _rope_tables(s_len):
    half = MLA_ROPE // 2
    inv = ROPE_BASE ** (-jnp.arange(half, dtype=F32) / half)
    ang = jnp.arange(s_len).astype(F32)[:, None] * inv[None, :]
    cos, sin = jnp.cos(ang), jnp.sin(ang)
    tail = jnp.zeros((s_len, LANES - MLA_NOPE - MLA_ROPE), F32)
    cos_t = jnp.concatenate([jnp.ones((s_len, MLA_NOPE), F32), cos, cos, tail], axis=1)
    sin_t = jnp.concatenate([jnp.zeros((s_len, MLA_NOPE), F32), sin, sin, tail], axis=1)
    scale = (MLA_NOPE + MLA_ROPE) ** -0.5 * LOG2E
    return jnp.concatenate([cos_t * scale, sin_t * scale, cos_t, sin_t], axis=1)


def _pre(x3, g, w, bf, uqa, uqb, ukvk, ukvv, qn, kvn, lb, consts):
    b, s_len, _ = x3.shape
    tm = min(TOKEN_TILE, s_len)
    tril, selq, selk, rowq, rowk, rope = consts
    tok = lambda width: pl.BlockSpec((1, tm, width), lambda bi, si: (bi, si, 0))
    bf16_out = lambda width: jax.ShapeDtypeStruct((b, s_len, width), BF16)
    f32_out = lambda width: jax.ShapeDtypeStruct((b, s_len, width), F32)
    in_specs = [
        tok(D_MODEL), _const_spec((1, D_MODEL)), _const_spec((D_MODEL, PRE_W)), _const_spec((1, LANES)),
        _const_spec((tm, tm)), _const_spec(selq.shape), _const_spec(selk.shape),
        _const_spec(rowq.shape), _const_spec(rowk.shape),
        _const_spec((1, MLA_Q_RANK)), _const_spec((1, MLA_KV_RANK)),
        _const_spec(uqa.shape), _const_spec(uqb.shape), _const_spec(ukvk.shape), _const_spec(ukvv.shape),
        pl.BlockSpec((tm, 4 * LANES), lambda bi, si: (si, 0)),
        _const_spec((1, BRANCH_W)),
    ]
    widths = [(512, BF16), (512, BF16), (256, BF16), (512, BF16), (512, BF16), (256, BF16),
              (256, BF16), (256, BF16), (256, BF16),
              (256, BF16), (256, F32), (256, F32), (256, BF16), (256, BF16)]
    return pl.pallas_call(
        _pre_kernel,
        out_shape=[bf16_out(wd) if dt == BF16 else f32_out(wd) for wd, dt in widths],
        grid=(b, s_len // tm),
        in_specs=in_specs,
        out_specs=[tok(wd) for wd, _ in widths],
        scratch_shapes=[pltpu.VMEM((1, LANES), F32)],
        compiler_params=_params("arbitrary", "arbitrary"),
        name="pre_mix",
    )(x3, g.reshape(1, D_MODEL), w, bf, tril, selq, selk, rowq, rowk,
      qn.reshape(1, -1), kvn.reshape(1, -1), uqa, uqb, ukvk, ukvv, rope, lb.reshape(1, -1))


def _lane_mask(shape, ranges):
    lane = lax.broadcasted_iota(jnp.int32, shape, 1)
    m = None
    for lo, hi in ranges:
        r = (lane >= lo) & (lane < hi)
        m = r if m is None else (m | r)
    return m


def _softmax_attn_kernel(q_ref, k_ref, v_ref, o_ref, qm_ref, s_ref, m_ref, l_ref, a_ref, acc_ref, *, t, lanes_a,
                         lanes_b):
    qi = pl.program_id(2)
    q = q_ref[0]
    zero = jnp.zeros_like(q)
    qm_ref[0] = jnp.where(_lane_mask(q.shape, lanes_a), q, zero)
    qm_ref[1] = jnp.where(_lane_mask(q.shape, lanes_b), q, zero)
    m_ref[...] = jnp.full(m_ref.shape, -1e30, F32)
    l_ref[...] = jnp.zeros(l_ref.shape, F32)
    acc_ref[...] = jnp.zeros(acc_ref.shape, F32)

    def start(hd, j, diagonal):
        s = _dot_nt(qm_ref[hd], k_ref[0, pl.ds(pl.multiple_of(j * t, t), t), :])
        if diagonal:
            row = lax.broadcasted_iota(jnp.int32, (t, t), 0)
            col = lax.broadcasted_iota(jnp.int32, (t, t), 1)
            s = jnp.where(col <= row, s, -1e30)
        s_ref[hd] = s
        m = m_ref[hd]
        m_new = jnp.maximum(m, jnp.max(s, axis=-1, keepdims=True))
        a_ref[hd] = jnp.exp2(m - m_new)
        m_ref[hd] = m_new

    def finish(hd, j):
        p = jnp.exp2(s_ref[hd] - jnp.concatenate([m_ref[hd]] * (t // LANES), axis=1))
        alpha = a_ref[hd]
        l_ref[hd] = alpha * l_ref[hd] + jnp.sum(p, axis=-1, keepdims=True)
        acc_ref[hd] = alpha * acc_ref[hd] + _dot(p.astype(BF16), v_ref[0, pl.ds(pl.multiple_of(j * t, t), t), :])

    def body(j, carry):
        finish(0, j)
        start(1, j, False)
        finish(1, j)
        start(0, j + 1, False)
        return carry

    @pl.when(qi == 0)
    def _():
        start(0, 0, True)

    @pl.when(qi > 0)
    def _():
        start(0, 0, False)
        lax.fori_loop(0, qi - 1, body, 0)
        last = qi - 1
        finish(0, last)
        start(1, last, False)
        finish(1, last)
        start(0, qi, True)

    finish(0, qi)
    start(1, qi, True)
    finish(1, qi)
    first_half = _lane_mask((t, LANES), ((0, HEAD_DIM),))
    o_ref[0] = jnp.where(first_half, acc_ref[0] / l_ref[0], acc_ref[1] / l_ref[1]).astype(BF16)


def _softmax_attn(q, k, v, kw, lanes_a, lanes_b, name):
    b, s_len, _ = q.shape
    t = min(ATTN_TILE, s_len)
    stat = pltpu.VMEM((2, t, LANES), F32)
    return pl.pallas_call(
        functools.partial(_softmax_attn_kernel, t=t, lanes_a=lanes_a, lanes_b=lanes_b),
        out_shape=jax.ShapeDtypeStruct((b, s_len, BRANCH_W), BF16),
        grid=(b, 2, s_len // t),
        in_specs=[
            pl.BlockSpec((1, t, kw), lambda bi, hp, qi: (bi, qi, hp)),
            pl.BlockSpec((1, s_len, kw), lambda bi, hp, qi: (bi, 0, hp)),
            pl.BlockSpec((1, s_len, LANES), lambda bi, hp, qi: (bi, 0, hp)),
        ],
        out_specs=pl.BlockSpec((1, t, LANES), lambda bi, hp, qi: (bi, qi, hp)),
        scratch_shapes=[pltpu.VMEM((2, t, kw), BF16), pltpu.VMEM((2, t, t), F32), stat, stat, stat, stat],
        compiler_params=_params("parallel", "parallel", "arbitrary"),
        name=name,
    )(q, k, v)


def _sb_attn_kernel(q_ref, k_ref, v_ref, u_ref, o_ref, qm_ref, rest_ref, acc_ref, *, tk, nsub):
    tq = nsub * tk
    qi = pl.program_id(2)
    q = q_ref[0]
    zero = jnp.zeros_like(q)
    qm_ref[0] = jnp.where(_lane_mask(q.shape, ((0, HEAD_DIM),)), q, zero)
    qm_ref[1] = jnp.where(_lane_mask(q.shape, ((HEAD_DIM, 2 * HEAD_DIM),)), q, zero)
    rest_ref[...] = jnp.zeros(rest_ref.shape, F32)
    acc_ref[...] = jnp.zeros(acc_ref.shape, F32)

    def weights(hd, r0, nrows, blk, diag):
        rows = pl.ds(r0, nrows)
        off = pl.multiple_of(blk * tk, tk)
        z = _dot_nt(qm_ref[hd, rows, :], k_ref[0, pl.ds(off, tk), :])
        neg_abs = pltpu.bitcast(pltpu.bitcast(z, jnp.uint32) | jnp.uint32(0x80000000), F32)
        log_beta = jnp.minimum(z, 0.0) - jnp.log2(1.0 + jnp.exp2(neg_abs))
        log_rest = log_beta - z
        if diag:
            row = lax.broadcasted_iota(jnp.int32, (tk, tk), 0)
            col = lax.broadcasted_iota(jnp.int32, (tk, tk), 1)
            strict = col < row
            log_rest = jnp.where(strict, log_rest, 0.0)
        rest = rest_ref[hd, rows, :]
        later = _dot(log_rest.astype(BF16), u_ref[...]) + jnp.concatenate([rest] * (tk // LANES), axis=1)
        a = jnp.exp2(log_beta + later)
        if diag:
            a = jnp.where(strict, a, 0.0)
        rest_ref[hd, rows, :] = rest + jnp.sum(log_rest, axis=-1, keepdims=True)
        return a.astype(BF16)

    def block(hd, r0, nrows, blk, diag):
        a = weights(hd, r0, nrows, blk, diag)
        acc_ref[hd, pl.ds(r0, nrows), :] += _dot(a, v_ref[0, pl.ds(pl.multiple_of(blk * tk, tk), tk), :])

    for hd in range(2):
        for d in reversed(range(nsub)):
            block(hd, d * tk, tk, nsub * qi + d, True)
            if d < nsub - 1:
                block(hd, (d + 1) * tk, tq - (d + 1) * tk, nsub * qi + d, False)

    def body(j, carry):
        chunk = qi - 1 - j
        off = pl.multiple_of(chunk * tq, tq)
        for hd in range(2):
            parts = [None] * nsub
            for d in reversed(range(nsub)):
                parts[d] = weights(hd, 0, tq, nsub * chunk + d, False)
            acc_ref[hd] += _dot(jnp.concatenate(parts, axis=1), v_ref[0, pl.ds(off, tq), :])
        return carry

    lax.fori_loop(0, qi, body, 0)
    first_half = _lane_mask((tq, LANES), ((0, HEAD_DIM),))
    o_ref[0] = jnp.where(first_half, acc_ref[0], acc_ref[1]).astype(BF16)


def _sb_attn(q, k, v, u):
    b, s_len, _ = q.shape
    tk = u.shape[0]
    nsub = min(SB_SUBBLOCKS, s_len // tk)
    tq = nsub * tk
    return pl.pallas_call(
        functools.partial(_sb_attn_kernel, tk=tk, nsub=nsub),
        out_shape=jax.ShapeDtypeStruct((b, s_len, BRANCH_W), BF16),
        grid=(b, 2, s_len // tq),
        in_specs=[
            pl.BlockSpec((1, tq, LANES), lambda bi, hp, qi: (bi, qi, hp)),
            pl.BlockSpec((1, s_len, LANES), lambda bi, hp, qi: (bi, 0, hp)),
            pl.BlockSpec((1, s_len, LANES), lambda bi, hp, qi: (bi, 0, hp)),
            _const_spec((tk, tk)),
        ],
        out_specs=pl.BlockSpec((1, tq, LANES), lambda bi, hp, qi: (bi, qi, hp)),
        scratch_shapes=[pltpu.VMEM((2, tq, LANES), BF16), pltpu.VMEM((2, tq, LANES), F32),
                        pltpu.VMEM((2, tq, LANES), F32)],
        compiler_params=_params("parallel", "parallel", "arbitrary"),
        name="stick_breaking",
    )(q, k, v, u)


def _hgrn_kernel(q_ref, k_ref, lf_ref, v_ref, gate_ref, on_ref, btril_ref, bones_ref, o_ref, st_ref, *, ts):
    @pl.when(pl.program_id(1) == 0)
    def _():
        st_ref[...] = jnp.zeros_like(st_ref)

    c = HGRN_CHUNK
    w = BRANCH_W
    btril = btril_ref[...]
    bones = bones_ref[...]
    head_mask = bones.astype(F32)
    cum = None
    for piece in _split3(lf_ref[0]):
        part = _dot(btril, piece)
        cum = part if cum is None else cum + part

    sub = 8
    groups = [(g * sub, (g + 1) * sub) for g in range(c // sub)]
    causal = [(t0 + lax.broadcasted_iota(jnp.int32, (sub, ns, w), 0)) >= lax.broadcasted_iota(jnp.int32, (sub, ns, w), 1)
              for t0, ns in groups]
    outs = []
    for ci in range(ts // c):
        r = slice(ci * c, (ci + 1) * c)
        bc = cum[r]
        qc = q_ref[0, r, :].astype(F32)
        kc = k_ref[0, r, :]
        vc = v_ref[0, r, :].astype(F32)
        prods = []
        for (t0, ns), mask in zip(groups, causal):
            bt = bc[t0:t0 + sub]
            dec = jnp.exp2(jnp.where(mask, bt[:, None, :] - bc[None, :ns, :], -jnp.inf))
            prods.append((qc[t0:t0 + sub, None, :] * dec * kc[None, :ns, :]).reshape(sub * ns, w).astype(BF16))
        scores = _dot(jnp.concatenate(prods, axis=0), bones)
        o_parts = []
        row0 = 0
        for t0, ns in groups:
            sc = scores[row0:row0 + sub * ns].reshape(sub, ns, w)
            o_parts.append(jnp.sum(sc * vc[None, :ns, :], axis=1))
            row0 += sub * ns
        o_intra = jnp.concatenate(o_parts, axis=0)
        st = st_ref[...]
        o_inter = _dot_nt((qc * jnp.exp2(bc)).astype(BF16), st.astype(BF16))
        b_last = bc[c - 1:c, :]
        k_dec = (kc * jnp.exp2(b_last - bc)).astype(BF16)
        upd = _dot(vc.T.astype(BF16), k_dec)
        st_ref[...] = st * jnp.exp2(b_last) + upd * head_mask
        outs.append(o_intra + o_inter)
    o = jnp.concatenate(outs, axis=0)
    sq_hi = (o * o).astype(BF16)
    sq_lo = (o * o - sq_hi.astype(F32)).astype(BF16)
    ms = (_dot(sq_hi, bones) + _dot(sq_lo, bones)) * (1.0 / HEAD_DIM)
    y = o * lax.rsqrt(ms + EPS) * on_ref[...] * gate_ref[0].astype(F32)
    o_ref[0] = y.astype(BF16)


def _hgrn(q, k, lf, v, gate, out_norm, btril, bones):
    b, s_len, w = q.shape
    ts = btril.shape[0]
    tok = pl.BlockSpec((1, ts, w), lambda bi, si: (bi, si, 0))
    return pl.pallas_call(
        functools.partial(_hgrn_kernel, ts=ts),
        out_shape=jax.ShapeDtypeStruct((b, s_len, w), BF16),
        grid=(b, s_len // ts),
        in_specs=[tok, tok, tok, tok, tok, _const_spec((1, w)), _const_spec((ts, ts)), _const_spec((w, w))],
        out_specs=tok,
        scratch_shapes=[pltpu.VMEM((w, w), F32)],
        compiler_params=_params("arbitrary", "arbitrary"),
        name="hgrn2",
    )(q, k, lf, v, gate, out_norm.reshape(1, w), btril, bones)


def _merge_kernel(x_ref, ya_ref, yb_ref, yc_ref, yd_ref, g_ref, wgate_ref, wbr_ref, wout_ref, o_ref):
    x = x_ref[...]
    h = _rms(x, g_ref[...]).astype(BF16)
    mixed = None
    for m, y_ref in enumerate((ya_ref, yb_ref, yc_ref, yd_ref)):
        gate = jax.nn.sigmoid(_dot(h, wgate_ref[:, m * D_MODEL:(m + 1) * D_MODEL]))
        term = gate * _dot(y_ref[...], wbr_ref[m])
        mixed = term if mixed is None else mixed + term
    o_ref[...] = x + _dot(mixed.astype(BF16), wout_ref[...])


def _merge(x2, ys, g, wgate, wbr, wout):
    n = x2.shape[0]
    tm = min(TOKEN_TILE, n)
    ytok = pl.BlockSpec((tm, BRANCH_W), lambda i: (i, 0))
    return pl.pallas_call(
        _merge_kernel,
        out_shape=jax.ShapeDtypeStruct((n, D_MODEL), F32),
        grid=(n // tm,),
        in_specs=[pl.BlockSpec((tm, D_MODEL), lambda i: (i, 0)), ytok, ytok, ytok, ytok,
                  _const_spec((1, D_MODEL)), _const_spec((D_MODEL, N_BRANCH * D_MODEL)),
                  _const_spec((N_BRANCH, BRANCH_W, D_MODEL)), _const_spec((D_MODEL, D_MODEL))],
        out_specs=pl.BlockSpec((tm, D_MODEL), lambda i: (i, 0)),
        compiler_params=_params("parallel"),
        name="merge",
    )(x2, *ys, g.reshape(1, D_MODEL), wgate, wbr, wout)


def kernel(x, p, w_in, b_fox_f, mla_q_norm, w_mla_uq, mla_kv_norm, w_mla_ukv, hgrn_lb_logits, hgrn_out_norm,
           w_branch, w_out, ffn_a_wi, ffn_a_wo, ffn_b_wi, ffn_b_wo, w_ple_in, w_ple_gate, norms, final_norm):
    b, s_len, _ = x.shape
    n = b * s_len
    depth = w_in.shape[0]
    tm = min(TOKEN_TILE, s_len)
    t = min(SB_KEY_TILE, s_len)
    ts = min(HGRN_TILE, s_len)

    tril = jnp.tril(jnp.ones((tm, tm), F32)).astype(BF16)
    selq, selk, rowq, rowk = _fox_selectors()
    rope = _rope_tables(s_len)
    suffix = jnp.tril(jnp.ones((t, t), F32), -1).astype(BF16)
    chunk_id = np.arange(ts) // HGRN_CHUNK
    btril = jnp.asarray((chunk_id[:, None] == chunk_id[None, :]) & (np.arange(ts)[:, None] >= np.arange(ts)[None, :]),
                        BF16)
    head_id = np.arange(BRANCH_W) // HEAD_DIM
    bones = jnp.asarray(head_id[:, None] == head_id[None, :], BF16)

    lb_cum = jnp.cumsum(jax.nn.softmax(hgrn_lb_logits.astype(F32), axis=0), axis=0)
    lower_bounds = lb_cum - lb_cum[0]

    fox_a = ((0, HEAD_DIM), (LANES, LANES + 8))
    fox_b = ((HEAD_DIM, LANES), (LANES + 8, LANES + 16))
    mla_a = ((0, LANES),)
    mla_b = ((LANES, 2 * LANES),)

    x2 = x.reshape(n, D_MODEL)
    for i in range(depth):
        x2 = _ffn(x2, norms[i, 0], ffn_a_wi[i], ffn_a_wo[i])
        w, bf, uqa, uqb, ukvk, ukvv = _pre_weights(w_in[i], b_fox_f[i], w_mla_uq[i], w_mla_ukv[i])
        (fq, fk, fv, mq, mk, mv, sq, sk, sv, hq, hk, hlf, hv, hg) = _pre(
            x2.reshape(b, s_len, D_MODEL), norms[i, 1], w, bf, uqa, uqb, ukvk, ukvv,
            mla_q_norm[i], mla_kv_norm[i], lower_bounds[i], (tril, selq, selk, rowq, rowk, rope))
        y_a = _softmax_attn(fq, fk, fv, 2 * LANES, fox_a, fox_b, "fox")
        y_b = _softmax_attn(mq, mk, mv, 2 * LANES, mla_a, mla_b, "mla")
        y_c = _sb_attn(sq, sk, sv, suffix)
        y_d = _hgrn(hq, hk, hlf, hv, hg, hgrn_out_norm[i], btril, bones)
        ys = [y.reshape(n, BRANCH_W) for y in (y_a, y_b, y_c, y_d)]
        x2 = _merge(x2, ys, norms[i, 1], w_in[i][:, IN_OFFS[14]:].astype(BF16),
                    w_branch[i].astype(BF16), w_out[i].astype(BF16))
        x2 = _ffn(x2, norms[i, 2], ffn_b_wi[i], ffn_b_wo[i])
        x2 = _ple(x2, p[i].reshape(n, PLE_DIM), norms[i, 3], w_ple_gate[i], w_ple_in[i],
                  final_norm if i == depth - 1 else None)
    return x2.reshape(b, s_len, D_MODEL)
```

```python
import functools

import jax
import jax.numpy as jnp
import numpy as np
from jax import lax
from jax.experimental import pallas as pl
from jax.experimental.pallas import tpu as pltpu

F32 = jnp.float32
BF16 = jnp.bfloat16

D_MODEL = 1024
PLE_DIM = 256
D_FF = 2816
EPS = 1e-6
LOG2E = 1.4426950408889634
N_BRANCH = 4
N_HEADS = 4
HEAD_DIM = 64
BRANCH_W = 256
MLA_NOPE = 64
MLA_ROPE = 32
MLA_Q_RANK = 256
MLA_KV_RANK = 128
ROPE_BASE = 10000.0
HGRN_CHUNK = 32
LANES = 128

IN_SPLITS = (256, 256, 256, 4, 256, 128, 32, 256, 256, 256, 256, 256, 256, 256, 4096)
IN_OFFS = tuple(int(o) for o in np.cumsum((0,) + IN_SPLITS))

PRE_COLS = dict(
    fq=(0, 256), fk=(256, 512), fv=(512, 768), ff_mckv=(768, 1024),
    mcq=(1024, 1280), mkr_pair=(1280, 1536),
    sq=(1536, 1792), sk=(1792, 2048), sv=(2048, 2304),
    hq=(2304, 2560), hf=(2560, 2816), hi=(2816, 3072), hg=(3072, 3328),
)
PRE_W = 3328

FF_CHUNK = 256
TOKEN_TILE = 512
ATTN_TILE = 1024
SB_KEY_TILE = 256
SB_SUBBLOCKS = 4
HGRN_TILE = 256
VMEM_LIMIT = 56 * 1024 * 1024


def _dot(a, b):
    return jnp.dot(a, b, preferred_element_type=F32)


def _dot_nt(a, b):
    return lax.dot_general(a, b, (((1,), (1,)), ((), ())), preferred_element_type=F32)


def _rms(x, g):
    return x * lax.rsqrt(jnp.mean(x * x, axis=-1, keepdims=True) + EPS) * g


def _log_sigmoid(x):
    return jnp.minimum(x, 0.0) - jnp.log1p(jnp.exp(-jnp.abs(x)))


def _split3(x):
    hi = x.astype(BF16)
    r1 = x - hi.astype(F32)
    mid = r1.astype(BF16)
    lo = (r1 - mid.astype(F32)).astype(BF16)
    return hi, mid, lo


def _params(*sem):
    return pltpu.CompilerParams(dimension_semantics=sem, vmem_limit_bytes=VMEM_LIMIT)


def _const_spec(shape):
    nd = len(shape)
    return pl.BlockSpec(shape, lambda *_: (0,) * nd, pipeline_mode=pl.Buffered(1))


def _ffn_kernel(x_ref, g_ref, wg_ref, wu_ref, wo_ref, o_ref, acc_ref):
    x = x_ref[...]
    h = _rms(x, g_ref[...]).astype(BF16)
    for c in range(D_FF // FF_CHUNK):
        sl = slice(c * FF_CHUNK, (c + 1) * FF_CHUNK)
        g = _dot(h, wg_ref[:, sl])
        u = _dot(h, wu_ref[:, sl])
        a = (g * jax.nn.sigmoid(g) * u).astype(BF16)
        part = _dot(a, wo_ref[sl, :])
        if c == 0:
            acc_ref[...] = part
        else:
            acc_ref[...] += part
    o_ref[...] = x + 0.5 * acc_ref[...]


def _ffn(x2, g, wi, wo):
    n = x2.shape[0]
    tm = min(TOKEN_TILE, n)
    wg = wi[:, :D_FF].astype(BF16)
    wu = wi[:, D_FF:].astype(BF16)
    return pl.pallas_call(
        _ffn_kernel,
        out_shape=jax.ShapeDtypeStruct((n, D_MODEL), F32),
        grid=(n // tm,),
        in_specs=[
            pl.BlockSpec((tm, D_MODEL), lambda i: (i, 0)),
            _const_spec((1, D_MODEL)),
            _const_spec((D_MODEL, D_FF)),
            _const_spec((D_MODEL, D_FF)),
            _const_spec((D_FF, D_MODEL)),
        ],
        out_specs=pl.BlockSpec((tm, D_MODEL), lambda i: (i, 0)),
        scratch_shapes=[pltpu.VMEM((tm, D_MODEL), F32)],
        compiler_params=_params("parallel"),
        name="ffn",
    )(x2, g.reshape(1, D_MODEL), wg, wu, wo.astype(BF16))


def _ple_kernel(x_ref, p_ref, g_ref, wpg_ref, wpe_ref, *rest, final):
    o_ref = rest[-1]
    x = x_ref[...]
    h = _rms(x, g_ref[...]).astype(BF16)
    gate = jax.nn.sigmoid(_dot(h, wpg_ref[...]))
    y = x + gate * _dot(p_ref[...].astype(BF16), wpe_ref[...])
    if final:
        y = _rms(y, rest[0][...])
    o_ref[...] = y


def _ple(x2, p2, g, wpg, wpe, final_g=None):
    n = x2.shape[0]
    tm = min(TOKEN_TILE, n)
    final = final_g is not None
    in_specs = [
        pl.BlockSpec((tm, D_MODEL), lambda i: (i, 0)),
        pl.BlockSpec((tm, PLE_DIM), lambda i: (i, 0)),
        _const_spec((1, D_MODEL)),
        _const_spec((D_MODEL, D_MODEL)),
        _const_spec((PLE_DIM, D_MODEL)),
    ]
    args = [x2, p2, g.reshape(1, D_MODEL), wpg.astype(BF16), wpe.astype(BF16)]
    if final:
        in_specs.append(_const_spec((1, D_MODEL)))
        args.append(final_g.reshape(1, D_MODEL))
    return pl.pallas_call(
        functools.partial(_ple_kernel, final=final),
        out_shape=jax.ShapeDtypeStruct((n, D_MODEL), F32),
        grid=(n // tm,),
        in_specs=in_specs,
        out_specs=pl.BlockSpec((tm, D_MODEL), lambda i: (i, 0)),
        compiler_params=_params("parallel"),
        name="ple_final" if final else "ple",
    )(*args)


def _pre_kernel(x_ref, g_ref, w_ref, bf_ref, tril_ref, selq_ref, selk_ref, rowq_ref, rowk_ref,
                qn_ref, kvn_ref, wuqa_ref, wuqb_ref, wukvk_ref, wukvv_ref, rope_ref, lb_ref,
                fq_ref, fk_ref, fv_ref, mq_ref, mk_ref, mv_ref, sq_ref, sk_ref, sv_ref,
                hq_ref, hk_ref, hlf_ref, hv_ref, hg_ref, carry_ref):
    @pl.when(pl.program_id(1) == 0)
    def _():
        carry_ref[...] = jnp.zeros_like(carry_ref)

    h = _rms(x_ref[0], g_ref[...]).astype(BF16)

    def proj(name):
        lo, hi = PRE_COLS[name]
        return _dot(h, w_ref[:, lo:hi])

    ff_mckv = proj("ff_mckv")
    log_f = _log_sigmoid(ff_mckv[:, :LANES] + bf_ref[...])
    tril = tril_ref[...]
    c = carry_ref[...]
    for piece in _split3(log_f):
        c = c + _dot(tril, piece)
    carry_ref[...] = c[c.shape[0] - 1:, :]
    c_parts = jnp.concatenate(_split3(c * LOG2E), axis=-1)
    cq = (_dot(c_parts, selq_ref[...]) + rowq_ref[...]).astype(BF16)
    ck = (_dot(c_parts, selk_ref[...]) + rowk_ref[...]).astype(BF16)
    fq = (proj("fq") * LOG2E).astype(BF16)
    fk = proj("fk").astype(BF16)
    for p in range(2):
        pair = slice(p * LANES, (p + 1) * LANES)
        fq_ref[0, :, 2 * p * LANES:(2 * p + 1) * LANES] = fq[:, pair]
        fq_ref[0, :, (2 * p + 1) * LANES:(2 * p + 2) * LANES] = cq[:, pair]
        fk_ref[0, :, 2 * p * LANES:(2 * p + 1) * LANES] = fk[:, pair]
        fk_ref[0, :, (2 * p + 1) * LANES:(2 * p + 2) * LANES] = ck[:, pair]
    fv_ref[0] = proj("fv").astype(BF16)

    rope = rope_ref[...]
    cos_q, sin_q = rope[:, 0:LANES], rope[:, LANES:2 * LANES]
    cos_k, sin_k = rope[:, 2 * LANES:3 * LANES], rope[:, 3 * LANES:4 * LANES]
    cq_lat = _rms(proj("mcq"), qn_ref[...]).astype(BF16)
    qa = _dot(cq_lat, wuqa_ref[...])
    qb = _dot(cq_lat, wuqb_ref[...])
    ckv_lat = _rms(ff_mckv[:, LANES:], kvn_ref[...]).astype(BF16)
    k_nope = _dot(ckv_lat, wukvk_ref[...])
    mkr_pair = proj("mkr_pair")
    k_rope = mkr_pair[:, :LANES] * cos_k + mkr_pair[:, LANES:] * sin_k
    for hd in range(N_HEADS):
        sl = slice(hd * LANES, (hd + 1) * LANES)
        mq_ref[0, :, sl] = (qa[:, sl] * cos_q + qb[:, sl] * sin_q).astype(BF16)
        mk_ref[0, :, sl] = (k_nope[:, sl] + k_rope).astype(BF16)
    mv_ref[0] = _dot(ckv_lat, wukvv_ref[...]).astype(BF16)

    sq_ref[0] = (proj("sq") * LOG2E).astype(BF16)
    sk_ref[0] = proj("sk").astype(BF16)
    sv_ref[0] = proj("sv").astype(BF16)

    lb = lb_ref[...]
    f = lb + (1.0 - lb) * jax.nn.sigmoid(proj("hf"))
    hq_ref[0] = proj("hq").astype(BF16)
    hk_ref[0] = 1.0 - f
    hlf_ref[0] = jnp.log(f) * LOG2E
    hv_ref[0] = proj("hi").astype(BF16)
    hg = proj("hg")
    hg_ref[0] = (hg * jax.nn.sigmoid(hg)).astype(BF16)


def _pre_weights(w_in, b_f, w_uq, w_ukv):
    cols = lambda j: w_in[:, IN_OFFS[j]:IN_OFFS[j + 1]]
    zeros = lambda n: jnp.zeros((D_MODEL, n), F32)
    half = MLA_ROPE // 2
    mkr = cols(6)
    mkr_sw = jnp.concatenate([-mkr[:, half:], mkr[:, :half]], axis=1)
    pad_rope = lambda t: jnp.concatenate([zeros(MLA_NOPE), t, zeros(LANES - MLA_NOPE - MLA_ROPE)], axis=1)
    scale = HEAD_DIM ** -0.5
    w = jnp.concatenate([
        cols(0) * scale, cols(1), cols(2),
        jnp.concatenate([cols(3), zeros(LANES - N_HEADS)], axis=1),
        cols(5), cols(4), pad_rope(mkr), pad_rope(mkr_sw),
        cols(7) * scale, cols(8), cols(9),
        cols(10), cols(11), cols(12), cols(13)], axis=1).astype(BF16)
    bf = jnp.concatenate([b_f, jnp.zeros((LANES - N_HEADS,), F32)]).reshape(1, LANES)

    dq = MLA_NOPE + MLA_ROPE
    uq = w_uq.reshape(MLA_Q_RANK, N_HEADS, dq)
    zq = jnp.zeros((MLA_Q_RANK, N_HEADS, LANES - dq), F32)
    uqa = jnp.concatenate([uq, zq], axis=2).reshape(MLA_Q_RANK, N_HEADS * LANES)
    rope_cols = uq[:, :, MLA_NOPE:]
    rope_sw = jnp.concatenate([-rope_cols[:, :, half:], rope_cols[:, :, :half]], axis=2)
    uqb = jnp.concatenate([jnp.zeros((MLA_Q_RANK, N_HEADS, MLA_NOPE), F32), rope_sw, zq],
                          axis=2).reshape(MLA_Q_RANK, N_HEADS * LANES)
    ukv = w_ukv.reshape(MLA_KV_RANK, N_HEADS, MLA_NOPE + HEAD_DIM)
    ukvk = jnp.concatenate([ukv[:, :, :MLA_NOPE], jnp.zeros((MLA_KV_RANK, N_HEADS, LANES - MLA_NOPE), F32)],
                           axis=2).reshape(MLA_KV_RANK, N_HEADS * LANES)
    ukvv = ukv[:, :, MLA_NOPE:].reshape(MLA_KV_RANK, N_HEADS * HEAD_DIM)
    return w, bf, uqa.astype(BF16), uqb.astype(BF16), ukvk.astype(BF16), ukvv.astype(BF16)


def _fox_selectors():
    selq = np.zeros((3 * LANES, 2 * LANES), np.float32)
    selk = np.zeros((3 * LANES, 2 * LANES), np.float32)
    rowq = np.zeros((1, 2 * LANES), np.float32)
    rowk = np.zeros((1, 2 * LANES), np.float32)
    for hd in range(N_HEADS):
        base = (hd // 2) * LANES + (hd % 2) * 8
        for piece in range(3):
            selq[piece * LANES + hd, base + piece] = 1.0
            selk[piece * LANES + hd, base + 3 + piece] = -1.0
            rowq[0, base + 3 + piece] = 1.0
            rowk[0, base + piece] = 1.0
    return (jnp.asarray(selq, BF16), jnp.asarray(selk, BF16), jnp.asarray(rowq), jnp.asarray(rowk))


def _rope_tables(s_len):
    half = MLA_ROPE // 2
    inv = ROPE_BASE ** (-jnp.arange(half, dtype=F32) / half)
    ang = jnp.arange(s_len).astype(F32)[:, None] * inv[None, :]
    cos, sin = jnp.cos(ang), jnp.sin(ang)
    tail = jnp.zeros((s_len, LANES - MLA_NOPE - MLA_ROPE), F32)
    cos_t = jnp.concatenate([jnp.ones((s_len, MLA_NOPE), F32), cos, cos, tail], axis=1)
    sin_t = jnp.concatenate([jnp.zeros((s_len, MLA_NOPE), F32), sin, sin, tail], axis=1)
    scale = (MLA_NOPE + MLA_ROPE) ** -0.5 * LOG2E
    return jnp.concatenate([cos_t * scale, sin_t * scale, cos_t, sin_t], axis=1)


def _pre(x3, g, w, bf, uqa, uqb, ukvk, ukvv, qn, kvn, lb, consts):
    b, s_len, _ = x3.shape
    tm = min(TOKEN_TILE, s_len)
    tril, selq, selk, rowq, rowk, rope = consts
    tok = lambda width: pl.BlockSpec((1, tm, width), lambda bi, si: (bi, si, 0))
    bf16_out = lambda width: jax.ShapeDtypeStruct((b, s_len, width), BF16)
    f32_out = lambda width: jax.ShapeDtypeStruct((b, s_len, width), F32)
    in_specs = [
        tok(D_MODEL), _const_spec((1, D_MODEL)), _const_spec((D_MODEL, PRE_W)), _const_spec((1, LANES)),
        _const_spec((tm, tm)), _const_spec(selq.shape), _const_spec(selk.shape),
        _const_spec(rowq.shape), _const_spec(rowk.shape),
        _const_spec((1, MLA_Q_RANK)), _const_spec((1, MLA_KV_RANK)),
        _const_spec(uqa.shape), _const_spec(uqb.shape), _const_spec(ukvk.shape), _const_spec(ukvv.shape),
        pl.BlockSpec((tm, 4 * LANES), lambda bi, si: (si, 0)),
        _const_spec((1, BRANCH_W)),
    ]
    widths = [(512, BF16), (512, BF16), (256, BF16), (512, BF16), (512, BF16), (256, BF16),
              (256, BF16), (256, BF16), (256, BF16),
              (256, BF16), (256, F32), (256, F32), (256, BF16), (256, BF16)]
    return pl.pallas_call(
        _pre_kernel,
        out_shape=[bf16_out(wd) if dt == BF16 else f32_out(wd) for wd, dt in widths],
        grid=(b, s_len // tm),
        in_specs=in_specs,
        out_specs=[tok(wd) for wd, _ in widths],
        scratch_shapes=[pltpu.VMEM((1, LANES), F32)],
        compiler_params=_params("arbitrary", "arbitrary"),
        name="pre_mix",
    )(x3, g.reshape(1, D_MODEL), w, bf, tril, selq, selk, rowq, rowk,
      qn.reshape(1, -1), kvn.reshape(1, -1), uqa, uqb, ukvk, ukvv, rope, lb.reshape(1, -1))


def _lane_mask(shape, ranges):
    lane = lax.broadcasted_iota(jnp.int32, shape, 1)
    m = None
    for lo, hi in ranges:
        r = (lane >= lo) & (lane < hi)
        m = r if m is None else (m | r)
    return m


def _softmax_attn_kernel(q_ref, k_ref, v_ref, o_ref, qm_ref, s_ref, m_ref, l_ref, a_ref, acc_ref, *, t, lanes_a,
                         lanes_b):
    qi = pl.program_id(2)
    q = q_ref[0]
    zero = jnp.zeros_like(q)
    qm_ref[0] = jnp.where(_lane_mask(q.shape, lanes_a), q, zero)
    qm_ref[1] = jnp.where(_lane_mask(q.shape, lanes_b), q, zero)
    m_ref[...] = jnp.full(m_ref.shape, -1e30, F32)
    l_ref[...] = jnp.zeros(l_ref.shape, F32)
    acc_ref[...] = jnp.zeros(acc_ref.shape, F32)

    half = t // 2
    top, bottom = slice(0, half), slice(half, t)

    def logits(hd, rows, key0, nkeys, cols, triangle):
        s = _dot_nt(qm_ref[hd, rows, :], k_ref[0, pl.ds(pl.multiple_of(key0, half), nkeys), :])
        if triangle:
            row = lax.broadcasted_iota(jnp.int32, s.shape, 0)
            col = lax.broadcasted_iota(jnp.int32, s.shape, 1)
            s = jnp.where(col <= row, s, -1e30)
        s_ref[hd, rows, cols] = s
        return jnp.max(s, axis=-1, keepdims=True)

    def fold_max(hd, rows, row_max):
        m = m_ref[hd, rows, :]
        m_new = jnp.maximum(m, row_max)
        a_ref[hd, rows, :] = jnp.exp2(m - m_new)
        m_ref[hd, rows, :] = m_new

    def weigh(hd, rows, cols, key0, nkeys):
        width = cols.stop - cols.start
        p = jnp.exp2(s_ref[hd, rows, cols] - jnp.concatenate([m_ref[hd, rows, :]] * (width // LANES), axis=1))
        alpha = a_ref[hd, rows, :]
        l_ref[hd, rows, :] = alpha * l_ref[hd, rows, :] + jnp.sum(p, axis=-1, keepdims=True)
        vb = v_ref[0, pl.ds(pl.multiple_of(key0, half), nkeys), :]
        acc_ref[hd, rows, :] = alpha * acc_ref[hd, rows, :] + _dot(p.astype(BF16), vb)

    def start(hd, j, diagonal):
        if not diagonal:
            fold_max(hd, slice(0, t), logits(hd, slice(0, t), j * t, t, slice(0, t), False))
            return
        fold_max(hd, top, logits(hd, top, j * t, half, top, True))
        fold_max(hd, bottom, jnp.maximum(logits(hd, bottom, j * t, half, top, False),
                                         logits(hd, bottom, j * t + half, half, bottom, True)))

    def finish(hd, j, diagonal=False):
        if not diagonal:
            weigh(hd, slice(0, t), slice(0, t), j * t, t)
            return
        weigh(hd, top, top, j * t, half)
        weigh(hd, bottom, slice(0, t), j * t, t)

    def body(j, carry):
        finish(0, j)
        start(1, j, False)
        finish(1, j)
        start(0, j + 1, False)
        return carry

    @pl.when(qi == 0)
    def _():
        start(0, 0, True)

    @pl.when(qi > 0)
    def _():
        start(0, 0, False)
        lax.fori_loop(0, qi - 1, body, 0)
        last = qi - 1
        finish(0, last)
        start(1, last, False)
        finish(1, last)
        start(0, qi, True)

    finish(0, qi, True)
    start(1, qi, True)
    finish(1, qi, True)
    first_half = _lane_mask((t, LANES), ((0, HEAD_DIM),))
    o_ref[0] = jnp.where(first_half, acc_ref[0] / l_ref[0], acc_ref[1] / l_ref[1]).astype(BF16)


def _softmax_attn(q, k, v, kw, lanes_a, lanes_b, name):
    b, s_len, _ = q.shape
    t = min(ATTN_TILE, s_len)
    stat = pltpu.VMEM((2, t, LANES), F32)
    return pl.pallas_call(
        functools.partial(_softmax_attn_kernel, t=t, lanes_a=lanes_a, lanes_b=lanes_b),
        out_shape=jax.ShapeDtypeStruct((b, s_len, BRANCH_W), BF16),
        grid=(b, 2, s_len // t),
        in_specs=[
            pl.BlockSpec((1, t, kw), lambda bi, hp, qi: (bi, qi, hp)),
            pl.BlockSpec((1, s_len, kw), lambda bi, hp, qi: (bi, 0, hp)),
            pl.BlockSpec((1, s_len, LANES), lambda bi, hp, qi: (bi, 0, hp)),
        ],
        out_specs=pl.BlockSpec((1, t, LANES), lambda bi, hp, qi: (bi, qi, hp)),
        scratch_shapes=[pltpu.VMEM((2, t, kw), BF16), pltpu.VMEM((2, t, t), F32), stat, stat, stat, stat],
        compiler_params=_params("parallel", "parallel", "arbitrary"),
        name=name,
    )(q, k, v)


def _sb_attn_kernel(q_ref, k_ref, v_ref, u_ref, o_ref, qm_ref, rest_ref, acc_ref, *, tk, nsub):
    tq = nsub * tk
    qi = pl.program_id(2)
    q = q_ref[0]
    zero = jnp.zeros_like(q)
    qm_ref[0] = jnp.where(_lane_mask(q.shape, ((0, HEAD_DIM),)), q, zero)
    qm_ref[1] = jnp.where(_lane_mask(q.shape, ((HEAD_DIM, 2 * HEAD_DIM),)), q, zero)
    rest_ref[...] = jnp.zeros(rest_ref.shape, F32)
    acc_ref[...] = jnp.zeros(acc_ref.shape, F32)

    def weights(hd, r0, nrows, blk, diag):
        rows = pl.ds(r0, nrows)
        off = pl.multiple_of(blk * tk, tk)
        z = _dot_nt(qm_ref[hd, rows, :], k_ref[0, pl.ds(off, tk), :])
        neg_abs = pltpu.bitcast(pltpu.bitcast(z, jnp.uint32) | jnp.uint32(0x80000000), F32)
        log_beta = jnp.minimum(z, 0.0) - jnp.log2(1.0 + jnp.exp2(neg_abs))
        log_rest = log_beta - z
        if diag:
            row = lax.broadcasted_iota(jnp.int32, (tk, tk), 0)
            col = lax.broadcasted_iota(jnp.int32, (tk, tk), 1)
            strict = col < row
            log_rest = jnp.where(strict, log_rest, 0.0)
        rest = rest_ref[hd, rows, :]
        later = _dot(log_rest.astype(BF16), u_ref[...]) + jnp.concatenate([rest] * (tk // LANES), axis=1)
        a = jnp.exp2(log_beta + later)
        if diag:
            a = jnp.where(strict, a, 0.0)
        rest_ref[hd, rows, :] = rest + jnp.sum(log_rest, axis=-1, keepdims=True)
        return a.astype(BF16)

    def block(hd, r0, nrows, blk, diag):
        a = weights(hd, r0, nrows, blk, diag)
        acc_ref[hd, pl.ds(r0, nrows), :] += _dot(a, v_ref[0, pl.ds(pl.multiple_of(blk * tk, tk), tk), :])

    for hd in range(2):
        for d in reversed(range(nsub)):
            block(hd, d * tk, tk, nsub * qi + d, True)
            if d < nsub - 1:
                block(hd, (d + 1) * tk, tq - (d + 1) * tk, nsub * qi + d, False)

    def body(j, carry):
        chunk = qi - 1 - j
        off = pl.multiple_of(chunk * tq, tq)
        for hd in range(2):
            parts = [None] * nsub
            for d in reversed(range(nsub)):
                parts[d] = weights(hd, 0, tq, nsub * chunk + d, False)
            acc_ref[hd] += _dot(jnp.concatenate(parts, axis=1), v_ref[0, pl.ds(off, tq), :])
        return carry

    lax.fori_loop(0, qi, body, 0)
    first_half = _lane_mask((tq, LANES), ((0, HEAD_DIM),))
    o_ref[0] = jnp.where(first_half, acc_ref[0], acc_ref[1]).astype(BF16)


def _sb_attn(q, k, v, u):
    b, s_len, _ = q.shape
    tk = u.shape[0]
    nsub = min(SB_SUBBLOCKS, s_len // tk)
    tq = nsub * tk
    return pl.pallas_call(
        functools.partial(_sb_attn_kernel, tk=tk, nsub=nsub),
        out_shape=jax.ShapeDtypeStruct((b, s_len, BRANCH_W), BF16),
        grid=(b, 2, s_len // tq),
        in_specs=[
            pl.BlockSpec((1, tq, LANES), lambda bi, hp, qi: (bi, qi, hp)),
            pl.BlockSpec((1, s_len, LANES), lambda bi, hp, qi: (bi, 0, hp)),
            pl.BlockSpec((1, s_len, LANES), lambda bi, hp, qi: (bi, 0, hp)),
            _const_spec((tk, tk)),
        ],
        out_specs=pl.BlockSpec((1, tq, LANES), lambda bi, hp, qi: (bi, qi, hp)),
        scratch_shapes=[pltpu.VMEM((2, tq, LANES), BF16), pltpu.VMEM((2, tq, LANES), F32),
                        pltpu.VMEM((2, tq, LANES), F32)],
        compiler_params=_params("parallel", "parallel", "arbitrary"),
        name="stick_breaking",
    )(q, k, v, u)


def _hgrn_kernel(q_ref, k_ref, lf_ref, v_ref, gate_ref, on_ref, btril_ref, bones_ref, o_ref, st_ref, *, ts):
    @pl.when(pl.program_id(1) == 0)
    def _():
        st_ref[...] = jnp.zeros_like(st_ref)

    c = HGRN_CHUNK
    w = BRANCH_W
    btril = btril_ref[...]
    bones = bones_ref[...]
    head_mask = bones.astype(F32)
    cum = None
    for piece in _split3(lf_ref[0]):
        part = _dot(btril, piece)
        cum = part if cum is None else cum + part

    sub = 8
    groups = [(g * sub, (g + 1) * sub) for g in range(c // sub)]
    causal = [(t0 + lax.broadcasted_iota(jnp.int32, (sub, ns, w), 0)) >= lax.broadcasted_iota(jnp.int32, (sub, ns, w), 1)
              for t0, ns in groups]
    outs = []
    for ci in range(ts // c):
        r = slice(ci * c, (ci + 1) * c)
        bc = cum[r]
        qc = q_ref[0, r, :].astype(F32)
        kc = k_ref[0, r, :]
        vc = v_ref[0, r, :].astype(F32)
        prods = []
        for (t0, ns), mask in zip(groups, causal):
            bt = bc[t0:t0 + sub]
            dec = jnp.exp2(jnp.where(mask, bt[:, None, :] - bc[None, :ns, :], -jnp.inf))
            prods.append((qc[t0:t0 + sub, None, :] * dec * kc[None, :ns, :]).reshape(sub * ns, w).astype(BF16))
        scores = _dot(jnp.concatenate(prods, axis=0), bones)
        o_parts = []
        row0 = 0
        for t0, ns in groups:
            sc = scores[row0:row0 + sub * ns].reshape(sub, ns, w)
            o_parts.append(jnp.sum(sc * vc[None, :ns, :], axis=1))
            row0 += sub * ns
        o_intra = jnp.concatenate(o_parts, axis=0)
        st = st_ref[...]
        o_inter = _dot_nt((qc * jnp.exp2(bc)).astype(BF16), st.astype(BF16))
        b_last = bc[c - 1:c, :]
        k_dec = (kc * jnp.exp2(b_last - bc)).astype(BF16)
        upd = _dot(vc.T.astype(BF16), k_dec)
        st_ref[...] = st * jnp.exp2(b_last) + upd * head_mask
        outs.append(o_intra + o_inter)
    o = jnp.concatenate(outs, axis=0)
    sq_hi = (o * o).astype(BF16)
    sq_lo = (o * o - sq_hi.astype(F32)).astype(BF16)
    ms = (_dot(sq_hi, bones) + _dot(sq_lo, bones)) * (1.0 / HEAD_DIM)
    y = o * lax.rsqrt(ms + EPS) * on_ref[...] * gate_ref[0].astype(F32)
    o_ref[0] = y.astype(BF16)


def _hgrn(q, k, lf, v, gate, out_norm, btril, bones):
    b, s_len, w = q.shape
    ts = btril.shape[0]
    tok = pl.BlockSpec((1, ts, w), lambda bi, si: (bi, si, 0))
    return pl.pallas_call(
        functools.partial(_hgrn_kernel, ts=ts),
        out_shape=jax.ShapeDtypeStruct((b, s_len, w), BF16),
        grid=(b, s_len // ts),
        in_specs=[tok, tok, tok, tok, tok, _const_spec((1, w)), _const_spec((ts, ts)), _const_spec((w, w))],
        out_specs=tok,
        scratch_shapes=[pltpu.VMEM((w, w), F32)],
        compiler_params=_params("arbitrary", "arbitrary"),
        name="hgrn2",
    )(q, k, lf, v, gate, out_norm.reshape(1, w), btril, bones)


def _merge_kernel(x_ref, ya_ref, yb_ref, yc_ref, yd_ref, g_ref, wgate_ref, wbr_ref, wout_ref, o_ref):
    x = x_ref[...]
    h = _rms(x, g_ref[...]).astype(BF16)
    mixed = None
    for m, y_ref in enumerate((ya_ref, yb_ref, yc_ref, yd_ref)):
        gate = jax.nn.sigmoid(_dot(h, wgate_ref[:, m * D_MODEL:(m + 1) * D_MODEL]))
        term = gate * _dot(y_ref[...], wbr_ref[m])
        mixed = term if mixed is None else mixed + term
    o_ref[...] = x + _dot(mixed.astype(BF16), wout_ref[...])


def _merge(x2, ys, g, wgate, wbr, wout):
    n = x2.shape[0]
    tm = min(TOKEN_TILE, n)
    ytok = pl.BlockSpec((tm, BRANCH_W), lambda i: (i, 0))
    return pl.pallas_call(
        _merge_kernel,
        out_shape=jax.ShapeDtypeStruct((n, D_MODEL), F32),
        grid=(n // tm,),
        in_specs=[pl.BlockSpec((tm, D_MODEL), lambda i: (i, 0)), ytok, ytok, ytok, ytok,
                  _const_spec((1, D_MODEL)), _const_spec((D_MODEL, N_BRANCH * D_MODEL)),
                  _const_spec((N_BRANCH, BRANCH_W, D_MODEL)), _const_spec((D_MODEL, D_MODEL))],
        out_specs=pl.BlockSpec((tm, D_MODEL), lambda i: (i, 0)),
        compiler_params=_params("parallel"),
        name="merge",
    )(x2, *ys, g.reshape(1, D_MODEL), wgate, wbr, wout)


def kernel(x, p, w_in, b_fox_f, mla_q_norm, w_mla_uq, mla_kv_norm, w_mla_ukv, hgrn_lb_logits, hgrn_out_norm,
           w_branch, w_out, ffn_a_wi, ffn_a_wo, ffn_b_wi, ffn_b_wo, w_ple_in, w_ple_gate, norms, final_norm):
    b, s_len, _ = x.shape
    n = b * s_len
    depth = w_in.shape[0]
    tm = min(TOKEN_TILE, s_len)
    t = min(SB_KEY_TILE, s_len)
    ts = min(HGRN_TILE, s_len)

    tril = jnp.tril(jnp.ones((tm, tm), F32)).astype(BF16)
    selq, selk, rowq, rowk = _fox_selectors()
    rope = _rope_tables(s_len)
    suffix = jnp.tril(jnp.ones((t, t), F32), -1).astype(BF16)
    chunk_id = np.arange(ts) // HGRN_CHUNK
    btril = jnp.asarray((chunk_id[:, None] == chunk_id[None, :]) & (np.arange(ts)[:, None] >= np.arange(ts)[None, :]),
                        BF16)
    head_id = np.arange(BRANCH_W) // HEAD_DIM
    bones = jnp.asarray(head_id[:, None] == head_id[None, :], BF16)

    lb_cum = jnp.cumsum(jax.nn.softmax(hgrn_lb_logits.astype(F32), axis=0), axis=0)
    lower_bounds = lb_cum - lb_cum[0]

    fox_a = ((0, HEAD_DIM), (LANES, LANES + 8))
    fox_b = ((HEAD_DIM, LANES), (LANES + 8, LANES + 16))
    mla_a = ((0, LANES),)
    mla_b = ((LANES, 2 * LANES),)

    x2 = x.reshape(n, D_MODEL)
    for i in range(depth):
        x2 = _ffn(x2, norms[i, 0], ffn_a_wi[i], ffn_a_wo[i])
        w, bf, uqa, uqb, ukvk, ukvv = _pre_weights(w_in[i], b_fox_f[i], w_mla_uq[i], w_mla_ukv[i])
        (fq, fk, fv, mq, mk, mv, sq, sk, sv, hq, hk, hlf, hv, hg) = _pre(
            x2.reshape(b, s_len, D_MODEL), norms[i, 1], w, bf, uqa, uqb, ukvk, ukvv,
            mla_q_norm[i], mla_kv_norm[i], lower_bounds[i], (tril, selq, selk, rowq, rowk, rope))
        y_a = _softmax_attn(fq, fk, fv, 2 * LANES, fox_a, fox_b, "fox")
        y_b = _softmax_attn(mq, mk, mv, 2 * LANES, mla_a, mla_b, "mla")
        y_c = _sb_attn(sq, sk, sv, suffix)
        y_d = _hgrn(hq, hk, hlf, hv, hg, hgrn_out_norm[i], btril, bones)
        ys = [y.reshape(n, BRANCH_W) for y in (y_a, y_b, y_c, y_d)]
        x2 = _merge(x2, ys, norms[i, 1], w_in[i][:, IN_OFFS[14]:].astype(BF16),
                    w_branch[i].astype(BF16), w_out[i].astype(BF16))
        x2 = _ffn(x2, norms[i, 2], ffn_b_wi[i], ffn_b_wo[i])
        x2 = _ple(x2, p[i].reshape(n, PLE_DIM), norms[i, 3], w_ple_gate[i], w_ple_in[i],
                  final_norm if i == depth - 1 else None)
    return x2.reshape(b, s_len, D_MODEL)
```

```python
import functools

import jax
import jax.numpy as jnp
import numpy as np
from jax import lax
from jax.experimental import pallas as pl
from jax.experimental.pallas import tpu as pltpu

F32 = jnp.float32
BF16 = jnp.bfloat16

D_MODEL = 1024
PLE_DIM = 256
D_FF = 2816
EPS = 1e-6
LOG2E = 1.4426950408889634
N_BRANCH = 4
N_HEADS = 4
HEAD_DIM = 64
BRANCH_W = 256
MLA_NOPE = 64
MLA_ROPE = 32
MLA_Q_RANK = 256
MLA_KV_RANK = 128
ROPE_BASE = 10000.0
HGRN_CHUNK = 32
LANES = 128

IN_SPLITS = (256, 256, 256, 4, 256, 128, 32, 256, 256, 256, 256, 256, 256, 256, 4096)
IN_OFFS = tuple(int(o) for o in np.cumsum((0,) + IN_SPLITS))

PRE_COLS = dict(
    fq=(0, 256), fk=(256, 512), fv=(512, 768), ff_mckv=(768, 1024),
    mcq=(1024, 1280), mkr_pair=(1280, 1536),
    sq=(1536, 1792), sk=(1792, 2048), sv=(2048, 2304),
    hq=(2304, 2560), hf=(2560, 2816), hi=(2816, 3072), hg=(3072, 3328),
)
PRE_W = 3328

FF_CHUNK = 256
TOKEN_TILE = 512
ATTN_TILE = 1024
SB_KEY_TILE = 256
SB_SUBBLOCKS = 8
HGRN_TILE = 256
VMEM_LIMIT = 56 * 1024 * 1024


def _dot(a, b):
    return jnp.dot(a, b, preferred_element_type=F32)


def _dot_nt(a, b):
    return lax.dot_general(a, b, (((1,), (1,)), ((), ())), preferred_element_type=F32)


def _rms(x, g):
    return x * lax.rsqrt(jnp.mean(x * x, axis=-1, keepdims=True) + EPS) * g


def _log_sigmoid(x):
    return jnp.minimum(x, 0.0) - jnp.log1p(jnp.exp(-jnp.abs(x)))


def _split3(x):
    hi = x.astype(BF16)
    r1 = x - hi.astype(F32)
    mid = r1.astype(BF16)
    lo = (r1 - mid.astype(F32)).astype(BF16)
    return hi, mid, lo


def _params(*sem):
    return pltpu.CompilerParams(dimension_semantics=sem, vmem_limit_bytes=VMEM_LIMIT)


def _const_spec(shape):
    nd = len(shape)
    return pl.BlockSpec(shape, lambda *_: (0,) * nd, pipeline_mode=pl.Buffered(1))


def _ffn_kernel(x_ref, g_ref, wg_ref, wu_ref, wo_ref, o_ref, acc_ref):
    x = x_ref[...]
    h = _rms(x, g_ref[...]).astype(BF16)
    for c in range(D_FF // FF_CHUNK):
        sl = slice(c * FF_CHUNK, (c + 1) * FF_CHUNK)
        g = _dot(h, wg_ref[:, sl])
        u = _dot(h, wu_ref[:, sl])
        a = (g * jax.nn.sigmoid(g) * u).astype(BF16)
        part = _dot(a, wo_ref[sl, :])
        if c == 0:
            acc_ref[...] = part
        else:
            acc_ref[...] += part
    o_ref[...] = x + 0.5 * acc_ref[...]


def _ffn(x2, g, wi, wo):
    n = x2.shape[0]
    tm = min(TOKEN_TILE, n)
    wg = wi[:, :D_FF].astype(BF16)
    wu = wi[:, D_FF:].astype(BF16)
    return pl.pallas_call(
        _ffn_kernel,
        out_shape=jax.ShapeDtypeStruct((n, D_MODEL), F32),
        grid=(n // tm,),
        in_specs=[
            pl.BlockSpec((tm, D_MODEL), lambda i: (i, 0)),
            _const_spec((1, D_MODEL)),
            _const_spec((D_MODEL, D_FF)),
            _const_spec((D_MODEL, D_FF)),
            _const_spec((D_FF, D_MODEL)),
        ],
        out_specs=pl.BlockSpec((tm, D_MODEL), lambda i: (i, 0)),
        scratch_shapes=[pltpu.VMEM((tm, D_MODEL), F32)],
        compiler_params=_params("parallel"),
        name="ffn",
    )(x2, g.reshape(1, D_MODEL), wg, wu, wo.astype(BF16))


def _ple_kernel(x_ref, p_ref, g_ref, wpg_ref, wpe_ref, *rest, final):
    o_ref = rest[-1]
    x = x_ref[...]
    h = _rms(x, g_ref[...]).astype(BF16)
    gate = jax.nn.sigmoid(_dot(h, wpg_ref[...]))
    y = x + gate * _dot(p_ref[...].astype(BF16), wpe_ref[...])
    if final:
        y = _rms(y, rest[0][...])
    o_ref[...] = y


def _ple(x2, p2, g, wpg, wpe, final_g=None):
    n = x2.shape[0]
    tm = min(TOKEN_TILE, n)
    final = final_g is not None
    in_specs = [
        pl.BlockSpec((tm, D_MODEL), lambda i: (i, 0)),
        pl.BlockSpec((tm, PLE_DIM), lambda i: (i, 0)),
        _const_spec((1, D_MODEL)),
        _const_spec((D_MODEL, D_MODEL)),
        _const_spec((PLE_DIM, D_MODEL)),
    ]
    args = [x2, p2, g.reshape(1, D_MODEL), wpg.astype(BF16), wpe.astype(BF16)]
    if final:
        in_specs.append(_const_spec((1, D_MODEL)))
        args.append(final_g.reshape(1, D_MODEL))
    return pl.pallas_call(
        functools.partial(_ple_kernel, final=final),
        out_shape=jax.ShapeDtypeStruct((n, D_MODEL), F32),
        grid=(n // tm,),
        in_specs=in_specs,
        out_specs=pl.BlockSpec((tm, D_MODEL), lambda i: (i, 0)),
        compiler_params=_params("parallel"),
        name="ple_final" if final else "ple",
    )(*args)


def _pre_kernel(x_ref, g_ref, w_ref, bf_ref, tril_ref, selq_ref, selk_ref, rowq_ref, rowk_ref,
                qn_ref, kvn_ref, wuqa_ref, wuqb_ref, wukvk_ref, wukvv_ref, rope_ref, lb_ref,
                fq_ref, fk_ref, fv_ref, mq_ref, mk_ref, mv_ref, sq_ref, sk_ref, sv_ref,
                hq_ref, hk_ref, hlf_ref, hv_ref, hg_ref, carry_ref):
    @pl.when(pl.program_id(1) == 0)
    def _():
        carry_ref[...] = jnp.zeros_like(carry_ref)

    h = _rms(x_ref[0], g_ref[...]).astype(BF16)

    def proj(name):
        lo, hi = PRE_COLS[name]
        return _dot(h, w_ref[:, lo:hi])

    ff_mckv = proj("ff_mckv")
    log_f = _log_sigmoid(ff_mckv[:, :LANES] + bf_ref[...])
    tril = tril_ref[...]
    c = carry_ref[...]
    for piece in _split3(log_f):
        c = c + _dot(tril, piece)
    carry_ref[...] = c[c.shape[0] - 1:, :]
    c_parts = jnp.concatenate(_split3(c * LOG2E), axis=-1)
    cq = (_dot(c_parts, selq_ref[...]) + rowq_ref[...]).astype(BF16)
    ck = (_dot(c_parts, selk_ref[...]) + rowk_ref[...]).astype(BF16)
    fq = (proj("fq") * LOG2E).astype(BF16)
    fk = proj("fk").astype(BF16)
    for p in range(2):
        pair = slice(p * LANES, (p + 1) * LANES)
        fq_ref[0, :, 2 * p * LANES:(2 * p + 1) * LANES] = fq[:, pair]
        fq_ref[0, :, (2 * p + 1) * LANES:(2 * p + 2) * LANES] = cq[:, pair]
        fk_ref[0, :, 2 * p * LANES:(2 * p + 1) * LANES] = fk[:, pair]
        fk_ref[0, :, (2 * p + 1) * LANES:(2 * p + 2) * LANES] = ck[:, pair]
    fv_ref[0] = proj("fv").astype(BF16)

    rope = rope_ref[...]
    cos_q, sin_q = rope[:, 0:LANES], rope[:, LANES:2 * LANES]
    cos_k, sin_k = rope[:, 2 * LANES:3 * LANES], rope[:, 3 * LANES:4 * LANES]
    cq_lat = _rms(proj("mcq"), qn_ref[...]).astype(BF16)
    qa = _dot(cq_lat, wuqa_ref[...])
    qb = _dot(cq_lat, wuqb_ref[...])
    ckv_lat = _rms(ff_mckv[:, LANES:], kvn_ref[...]).astype(BF16)
    k_nope = _dot(ckv_lat, wukvk_ref[...])
    mkr_pair = proj("mkr_pair")
    k_rope = mkr_pair[:, :LANES] * cos_k + mkr_pair[:, LANES:] * sin_k
    for hd in range(N_HEADS):
        sl = slice(hd * LANES, (hd + 1) * LANES)
        mq_ref[0, :, sl] = (qa[:, sl] * cos_q + qb[:, sl] * sin_q).astype(BF16)
        mk_ref[0, :, sl] = (k_nope[:, sl] + k_rope).astype(BF16)
    mv_ref[0] = _dot(ckv_lat, wukvv_ref[...]).astype(BF16)

    sq_ref[0] = (proj("sq") * LOG2E).astype(BF16)
    sk_ref[0] = proj("sk").astype(BF16)
    sv_ref[0] = proj("sv").astype(BF16)

    lb = lb_ref[...]
    f = lb + (1.0 - lb) * jax.nn.sigmoid(proj("hf"))
    hq_ref[0] = proj("hq").astype(BF16)
    hk_ref[0] = 1.0 - f
    hlf_ref[0] = jnp.log(f) * LOG2E
    hv_ref[0] = proj("hi").astype(BF16)
    hg = proj("hg")
    hg_ref[0] = (hg * jax.nn.sigmoid(hg)).astype(BF16)


def _pre_weights(w_in, b_f, w_uq, w_ukv):
    cols = lambda j: w_in[:, IN_OFFS[j]:IN_OFFS[j + 1]]
    zeros = lambda n: jnp.zeros((D_MODEL, n), F32)
    half = MLA_ROPE // 2
    mkr = cols(6)
    mkr_sw = jnp.concatenate([-mkr[:, half:], mkr[:, :half]], axis=1)
    pad_rope = lambda t: jnp.concatenate([zeros(MLA_NOPE), t, zeros(LANES - MLA_NOPE - MLA_ROPE)], axis=1)
    scale = HEAD_DIM ** -0.5
    w = jnp.concatenate([
        cols(0) * scale, cols(1), cols(2),
        jnp.concatenate([cols(3), zeros(LANES - N_HEADS)], axis=1),
        cols(5), cols(4), pad_rope(mkr), pad_rope(mkr_sw),
        cols(7) * scale, cols(8), cols(9),
        cols(10), cols(11), cols(12), cols(13)], axis=1).astype(BF16)
    bf = jnp.concatenate([b_f, jnp.zeros((LANES - N_HEADS,), F32)]).reshape(1, LANES)

    dq = MLA_NOPE + MLA_ROPE
    uq = w_uq.reshape(MLA_Q_RANK, N_HEADS, dq)
    zq = jnp.zeros((MLA_Q_RANK, N_HEADS, LANES - dq), F32)
    uqa = jnp.concatenate([uq, zq], axis=2).reshape(MLA_Q_RANK, N_HEADS * LANES)
    rope_cols = uq[:, :, MLA_NOPE:]
    rope_sw = jnp.concatenate([-rope_cols[:, :, half:], rope_cols[:, :, :half]], axis=2)
    uqb = jnp.concatenate([jnp.zeros((MLA_Q_RANK, N_HEADS, MLA_NOPE), F32), rope_sw, zq],
                          axis=2).reshape(MLA_Q_RANK, N_HEADS * LANES)
    ukv = w_ukv.reshape(MLA_KV_RANK, N_HEADS, MLA_NOPE + HEAD_DIM)
    ukvk = jnp.concatenate([ukv[:, :, :MLA_NOPE], jnp.zeros((MLA_KV_RANK, N_HEADS, LANES - MLA_NOPE), F32)],
                           axis=2).reshape(MLA_KV_RANK, N_HEADS * LANES)
    ukvv = ukv[:, :, MLA_NOPE:].reshape(MLA_KV_RANK, N_HEADS * HEAD_DIM)
    return w, bf, uqa.astype(BF16), uqb.astype(BF16), ukvk.astype(BF16), ukvv.astype(BF16)


def _fox_selectors():
    selq = np.zeros((3 * LANES, 2 * LANES), np.float32)
    selk = np.zeros((3 * LANES, 2 * LANES), np.float32)
    rowq = np.zeros((1, 2 * LANES), np.float32)
    rowk = np.zeros((1, 2 * LANES), np.float32)
    for hd in range(N_HEADS):
        base = (hd // 2) * LANES + (hd % 2) * 8
        for piece in range(3):
            selq[piece * LANES + hd, base + piece] = 1.0
            selk[piece * LANES + hd, base + 3 + piece] = -1.0
            rowq[0, base + 3 + piece] = 1.0
            rowk[0, base + piece] = 1.0
    return (jnp.asarray(selq, BF16), jnp.asarray(selk, BF16), jnp.asarray(rowq), jnp.asarray(rowk))


def _rope_tables(s_len):
    half = MLA_ROPE // 2
    inv = ROPE_BASE ** (-jnp.arange(half, dtype=F32) / half)
    ang = jnp.arange(s_len).astype(F32)[:, None] * inv[None, :]
    cos, sin = jnp.cos(ang), jnp.sin(ang)
    tail = jnp.zeros((s_len, LANES - MLA_NOPE - MLA_ROPE), F32)
    cos_t = jnp.concatenate([jnp.ones((s_len, MLA_NOPE), F32), cos, cos, tail], axis=1)
    sin_t = jnp.concatenate([jnp.zeros((s_len, MLA_NOPE), F32), sin, sin, tail], axis=1)
    scale = (MLA_NOPE + MLA_ROPE) ** -0.5 * LOG2E
    return jnp.concatenate([cos_t * scale, sin_t * scale, cos_t, sin_t], axis=1)


def _pre(x3, g, w, bf, uqa, uqb, ukvk, ukvv, qn, kvn, lb, consts):
    b, s_len, _ = x3.shape
    tm = min(TOKEN_TILE, s_len)
    tril, selq, selk, rowq, rowk, rope = consts
    tok = lambda width: pl.BlockSpec((1, tm, width), lambda bi, si: (bi, si, 0))
    bf16_out = lambda width: jax.ShapeDtypeStruct((b, s_len, width), BF16)
    f32_out = lambda width: jax.ShapeDtypeStruct((b, s_len, width), F32)
    in_specs = [
        tok(D_MODEL), _const_spec((1, D_MODEL)), _const_spec((D_MODEL, PRE_W)), _const_spec((1, LANES)),
        _const_spec((tm, tm)), _const_spec(selq.shape), _const_spec(selk.shape),
        _const_spec(rowq.shape), _const_spec(rowk.shape),
        _const_spec((1, MLA_Q_RANK)), _const_spec((1, MLA_KV_RANK)),
        _const_spec(uqa.shape), _const_spec(uqb.shape), _const_spec(ukvk.shape), _const_spec(ukvv.shape),
        pl.BlockSpec((tm, 4 * LANES), lambda bi, si: (si, 0)),
        _const_spec((1, BRANCH_W)),
    ]
    widths = [(512, BF16), (512, BF16), (256, BF16), (512, BF16), (512, BF16), (256, BF16),
              (256, BF16), (256, BF16), (256, BF16),
              (256, BF16), (256, F32), (256, F32), (256, BF16), (256, BF16)]
    return pl.pallas_call(
        _pre_kernel,
        out_shape=[bf16_out(wd) if dt == BF16 else f32_out(wd) for wd, dt in widths],
        grid=(b, s_len // tm),
        in_specs=in_specs,
        out_specs=[tok(wd) for wd, _ in widths],
        scratch_shapes=[pltpu.VMEM((1, LANES), F32)],
        compiler_params=_params("arbitrary", "arbitrary"),
        name="pre_mix",
    )(x3, g.reshape(1, D_MODEL), w, bf, tril, selq, selk, rowq, rowk,
      qn.reshape(1, -1), kvn.reshape(1, -1), uqa, uqb, ukvk, ukvv, rope, lb.reshape(1, -1))


def _lane_mask(shape, ranges):
    lane = lax.broadcasted_iota(jnp.int32, shape, 1)
    m = None
    for lo, hi in ranges:
        r = (lane >= lo) & (lane < hi)
        m = r if m is None else (m | r)
    return m


def _softmax_attn_kernel(q_ref, k_ref, v_ref, o_ref, qm_ref, s_ref, m_ref, l_ref, a_ref, acc_ref, *, t, lanes_a,
                         lanes_b):
    qi = pl.program_id(2)
    q = q_ref[0]
    zero = jnp.zeros_like(q)
    qm_ref[0] = jnp.where(_lane_mask(q.shape, lanes_a), q, zero)
    qm_ref[1] = jnp.where(_lane_mask(q.shape, lanes_b), q, zero)
    m_ref[...] = jnp.full(m_ref.shape, -1e30, F32)
    l_ref[...] = jnp.zeros(l_ref.shape, F32)
    acc_ref[...] = jnp.zeros(acc_ref.shape, F32)

    half = t // 2
    top, bottom = slice(0, half), slice(half, t)

    def logits(hd, rows, key0, nkeys, cols, triangle):
        s = _dot_nt(qm_ref[hd, rows, :], k_ref[0, pl.ds(pl.multiple_of(key0, half), nkeys), :])
        if triangle:
            row = lax.broadcasted_iota(jnp.int32, s.shape, 0)
            col = lax.broadcasted_iota(jnp.int32, s.shape, 1)
            s = jnp.where(col <= row, s, -1e30)
        s_ref[hd, rows, cols] = s
        return jnp.max(s, axis=-1, keepdims=True)

    def fold_max(hd, rows, row_max):
        m = m_ref[hd, rows, :]
        m_new = jnp.maximum(m, row_max)
        a_ref[hd, rows, :] = jnp.exp2(m - m_new)
        m_ref[hd, rows, :] = m_new

    def weigh(hd, rows, cols, key0, nkeys):
        width = cols.stop - cols.start
        p = jnp.exp2(s_ref[hd, rows, cols] - jnp.concatenate([m_ref[hd, rows, :]] * (width // LANES), axis=1))
        alpha = a_ref[hd, rows, :]
        l_ref[hd, rows, :] = alpha * l_ref[hd, rows, :] + jnp.sum(p, axis=-1, keepdims=True)
        vb = v_ref[0, pl.ds(pl.multiple_of(key0, half), nkeys), :]
        acc_ref[hd, rows, :] = alpha * acc_ref[hd, rows, :] + _dot(p.astype(BF16), vb)

    def start(hd, j, diagonal):
        if not diagonal:
            fold_max(hd, slice(0, t), logits(hd, slice(0, t), j * t, t, slice(0, t), False))
            return
        fold_max(hd, top, logits(hd, top, j * t, half, top, True))
        fold_max(hd, bottom, jnp.maximum(logits(hd, bottom, j * t, half, top, False),
                                         logits(hd, bottom, j * t + half, half, bottom, True)))

    def finish(hd, j, diagonal=False):
        if not diagonal:
            weigh(hd, slice(0, t), slice(0, t), j * t, t)
            return
        weigh(hd, top, top, j * t, half)
        weigh(hd, bottom, slice(0, t), j * t, t)

    def body(j, carry):
        finish(0, j)
        start(1, j, False)
        finish(1, j)
        start(0, j + 1, False)
        return carry

    @pl.when(qi == 0)
    def _():
        start(0, 0, True)

    @pl.when(qi > 0)
    def _():
        start(0, 0, False)
        lax.fori_loop(0, qi - 1, body, 0)
        last = qi - 1
        finish(0, last)
        start(1, last, False)
        finish(1, last)
        start(0, qi, True)

    finish(0, qi, True)
    start(1, qi, True)
    finish(1, qi, True)
    first_half = _lane_mask((t, LANES), ((0, HEAD_DIM),))
    o_ref[0] = jnp.where(first_half, acc_ref[0] / l_ref[0], acc_ref[1] / l_ref[1]).astype(BF16)


def _softmax_attn(q, k, v, kw, lanes_a, lanes_b, name):
    b, s_len, _ = q.shape
    t = min(ATTN_TILE, s_len)
    stat = pltpu.VMEM((2, t, LANES), F32)
    return pl.pallas_call(
        functools.partial(_softmax_attn_kernel, t=t, lanes_a=lanes_a, lanes_b=lanes_b),
        out_shape=jax.ShapeDtypeStruct((b, s_len, BRANCH_W), BF16),
        grid=(b, 2, s_len // t),
        in_specs=[
            pl.BlockSpec((1, t, kw), lambda bi, hp, qi: (bi, qi, hp)),
            pl.BlockSpec((1, s_len, kw), lambda bi, hp, qi: (bi, 0, hp)),
            pl.BlockSpec((1, s_len, LANES), lambda bi, hp, qi: (bi, 0, hp)),
        ],
        out_specs=pl.BlockSpec((1, t, LANES), lambda bi, hp, qi: (bi, qi, hp)),
        scratch_shapes=[pltpu.VMEM((2, t, kw), BF16), pltpu.VMEM((2, t, t), F32), stat, stat, stat, stat],
        compiler_params=_params("parallel", "parallel", "arbitrary"),
        name=name,
    )(q, k, v)


def _sb_attn_kernel(q_ref, k_ref, v_ref, u_ref, o_ref, qm_ref, rest_ref, acc_ref, *, tk, nsub):
    tq = nsub * tk
    qi = pl.program_id(2)
    q = q_ref[0]
    zero = jnp.zeros_like(q)
    qm_ref[0] = jnp.where(_lane_mask(q.shape, ((0, HEAD_DIM),)), q, zero)
    qm_ref[1] = jnp.where(_lane_mask(q.shape, ((HEAD_DIM, 2 * HEAD_DIM),)), q, zero)
    rest_ref[...] = jnp.zeros(rest_ref.shape, F32)
    acc_ref[...] = jnp.zeros(acc_ref.shape, F32)

    def weights(hd, r0, nrows, blk, diag):
        rows = pl.ds(r0, nrows)
        off = pl.multiple_of(blk * tk, tk)
        z = _dot_nt(qm_ref[hd, rows, :], k_ref[0, pl.ds(off, tk), :])
        neg_abs = pltpu.bitcast(pltpu.bitcast(z, jnp.uint32) | jnp.uint32(0x80000000), F32)
        log_beta = jnp.minimum(z, 0.0) - jnp.log2(1.0 + jnp.exp2(neg_abs))
        log_rest = log_beta - z
        if diag:
            row = lax.broadcasted_iota(jnp.int32, (tk, tk), 0)
            col = lax.broadcasted_iota(jnp.int32, (tk, tk), 1)
            strict = col < row
            log_rest = jnp.where(strict, log_rest, 0.0)
        rest = rest_ref[hd, rows, :]
        later = _dot(log_rest.astype(BF16), u_ref[...]) + jnp.concatenate([rest] * (tk // LANES), axis=1)
        a = jnp.exp2(log_beta + later)
        if diag:
            a = jnp.where(strict, a, 0.0)
        rest_ref[hd, rows, :] = rest + jnp.sum(log_rest, axis=-1, keepdims=True)
        return a.astype(BF16)

    def block(hd, r0, nrows, blk, diag):
        a = weights(hd, r0, nrows, blk, diag)
        acc_ref[hd, pl.ds(r0, nrows), :] += _dot(a, v_ref[0, pl.ds(pl.multiple_of(blk * tk, tk), tk), :])

    for hd in range(2):
        for d in reversed(range(nsub)):
            block(hd, d * tk, tk, nsub * qi + d, True)
            if d < nsub - 1:
                block(hd, (d + 1) * tk, tq - (d + 1) * tk, nsub * qi + d, False)

    def body(j, carry):
        chunk = qi - 1 - j
        off = pl.multiple_of(chunk * tq, tq)
        for hd in range(2):
            parts = [None] * nsub
            for d in reversed(range(nsub)):
                parts[d] = weights(hd, 0, tq, nsub * chunk + d, False)
            acc_ref[hd] += _dot(jnp.concatenate(parts, axis=1), v_ref[0, pl.ds(off, tq), :])
        return carry

    lax.fori_loop(0, qi, body, 0)
    first_half = _lane_mask((tq, LANES), ((0, HEAD_DIM),))
    o_ref[0] = jnp.where(first_half, acc_ref[0], acc_ref[1]).astype(BF16)


def _sb_attn(q, k, v, u):
    b, s_len, _ = q.shape
    tk = u.shape[0]
    nsub = min(SB_SUBBLOCKS, s_len // tk)
    tq = nsub * tk
    return pl.pallas_call(
        functools.partial(_sb_attn_kernel, tk=tk, nsub=nsub),
        out_shape=jax.ShapeDtypeStruct((b, s_len, BRANCH_W), BF16),
        grid=(b, 2, s_len // tq),
        in_specs=[
            pl.BlockSpec((1, tq, LANES), lambda bi, hp, qi: (bi, qi, hp)),
            pl.BlockSpec((1, s_len, LANES), lambda bi, hp, qi: (bi, 0, hp)),
            pl.BlockSpec((1, s_len, LANES), lambda bi, hp, qi: (bi, 0, hp)),
            _const_spec((tk, tk)),
        ],
        out_specs=pl.BlockSpec((1, tq, LANES), lambda bi, hp, qi: (bi, qi, hp)),
        scratch_shapes=[pltpu.VMEM((2, tq, LANES), BF16), pltpu.VMEM((2, tq, LANES), F32),
                        pltpu.VMEM((2, tq, LANES), F32)],
        compiler_params=_params("parallel", "parallel", "arbitrary"),
        name="stick_breaking",
    )(q, k, v, u)


def _hgrn_kernel(q_ref, k_ref, lf_ref, v_ref, gate_ref, on_ref, btril_ref, bones_ref, o_ref, st_ref, *, ts):
    @pl.when(pl.program_id(1) == 0)
    def _():
        st_ref[...] = jnp.zeros_like(st_ref)

    c = HGRN_CHUNK
    w = BRANCH_W
    btril = btril_ref[...]
    bones = bones_ref[...]
    head_mask = bones.astype(F32)
    cum = None
    for piece in _split3(lf_ref[0]):
        part = _dot(btril, piece)
        cum = part if cum is None else cum + part

    sub = 8
    groups = [(g * sub, (g + 1) * sub) for g in range(c // sub)]
    causal = [(t0 + lax.broadcasted_iota(jnp.int32, (sub, ns, w), 0)) >= lax.broadcasted_iota(jnp.int32, (sub, ns, w), 1)
              for t0, ns in groups]
    outs = []
    for ci in range(ts // c):
        r = slice(ci * c, (ci + 1) * c)
        bc = cum[r]
        qc = q_ref[0, r, :].astype(F32)
        kc = k_ref[0, r, :]
        vc = v_ref[0, r, :].astype(F32)
        prods = []
        for (t0, ns), mask in zip(groups, causal):
            bt = bc[t0:t0 + sub]
            dec = jnp.exp2(jnp.where(mask, bt[:, None, :] - bc[None, :ns, :], -jnp.inf))
            prods.append((qc[t0:t0 + sub, None, :] * dec * kc[None, :ns, :]).reshape(sub * ns, w).astype(BF16))
        scores = _dot(jnp.concatenate(prods, axis=0), bones)
        o_parts = []
        row0 = 0
        for t0, ns in groups:
            sc = scores[row0:row0 + sub * ns].reshape(sub, ns, w)
            o_parts.append(jnp.sum(sc * vc[None, :ns, :], axis=1))
            row0 += sub * ns
        o_intra = jnp.concatenate(o_parts, axis=0)
        st = st_ref[...]
        o_inter = _dot_nt((qc * jnp.exp2(bc)).astype(BF16), st.astype(BF16))
        b_last = bc[c - 1:c, :]
        k_dec = (kc * jnp.exp2(b_last - bc)).astype(BF16)
        upd = _dot(vc.T.astype(BF16), k_dec)
        st_ref[...] = st * jnp.exp2(b_last) + upd * head_mask
        outs.append(o_intra + o_inter)
    o = jnp.concatenate(outs, axis=0)
    sq_hi = (o * o).astype(BF16)
    sq_lo = (o * o - sq_hi.astype(F32)).astype(BF16)
    ms = (_dot(sq_hi, bones) + _dot(sq_lo, bones)) * (1.0 / HEAD_DIM)
    y = o * lax.rsqrt(ms + EPS) * on_ref[...] * gate_ref[0].astype(F32)
    o_ref[0] = y.astype(BF16)


def _hgrn(q, k, lf, v, gate, out_norm, btril, bones):
    b, s_len, w = q.shape
    ts = btril.shape[0]
    tok = pl.BlockSpec((1, ts, w), lambda bi, si: (bi, si, 0))
    return pl.pallas_call(
        functools.partial(_hgrn_kernel, ts=ts),
        out_shape=jax.ShapeDtypeStruct((b, s_len, w), BF16),
        grid=(b, s_len // ts),
        in_specs=[tok, tok, tok, tok, tok, _const_spec((1, w)), _const_spec((ts, ts)), _const_spec((w, w))],
        out_specs=tok,
        scratch_shapes=[pltpu.VMEM((w, w), F32)],
        compiler_params=_params("arbitrary", "arbitrary"),
        name="hgrn2",
    )(q, k, lf, v, gate, out_norm.reshape(1, w), btril, bones)


def _merge_kernel(x_ref, ya_ref, yb_ref, yc_ref, yd_ref, g_ref, wgate_ref, wbr_ref, wout_ref, o_ref):
    x = x_ref[...]
    h = _rms(x, g_ref[...]).astype(BF16)
    mixed = None
    for m, y_ref in enumerate((ya_ref, yb_ref, yc_ref, yd_ref)):
        gate = jax.nn.sigmoid(_dot(h, wgate_ref[:, m * D_MODEL:(m + 1) * D_MODEL]))
        term = gate * _dot(y_ref[...], wbr_ref[m])
        mixed = term if mixed is None else mixed + term
    o_ref[...] = x + _dot(mixed.astype(BF16), wout_ref[...])


def _merge(x2, ys, g, wgate, wbr, wout):
    n = x2.shape[0]
    tm = min(TOKEN_TILE, n)
    ytok = pl.BlockSpec((tm, BRANCH_W), lambda i: (i, 0))
    return pl.pallas_call(
        _merge_kernel,
        out_shape=jax.ShapeDtypeStruct((n, D_MODEL), F32),
        grid=(n // tm,),
        in_specs=[pl.BlockSpec((tm, D_MODEL), lambda i: (i, 0)), ytok, ytok, ytok, ytok,
                  _const_spec((1, D_MODEL)), _const_spec((D_MODEL, N_BRANCH * D_MODEL)),
                  _const_spec((N_BRANCH, BRANCH_W, D_MODEL)), _const_spec((D_MODEL, D_MODEL))],
        out_specs=pl.BlockSpec((tm, D_MODEL), lambda i: (i, 0)),
        compiler_params=_params("parallel"),
        name="merge",
    )(x2, *ys, g.reshape(1, D_MODEL), wgate, wbr, wout)


def kernel(x, p, w_in, b_fox_f, mla_q_norm, w_mla_uq, mla_kv_norm, w_mla_ukv, hgrn_lb_logits, hgrn_out_norm,
           w_branch, w_out, ffn_a_wi, ffn_a_wo, ffn_b_wi, ffn_b_wo, w_ple_in, w_ple_gate, norms, final_norm):
    b, s_len, _ = x.shape
    n = b * s_len
    depth = w_in.shape[0]
    tm = min(TOKEN_TILE, s_len)
    t = min(SB_KEY_TILE, s_len)
    ts = min(HGRN_TILE, s_len)

    tril = jnp.tril(jnp.ones((tm, tm), F32)).astype(BF16)
    selq, selk, rowq, rowk = _fox_selectors()
    rope = _rope_tables(s_len)
    suffix = jnp.tril(jnp.ones((t, t), F32), -1).astype(BF16)
    chunk_id = np.arange(ts) // HGRN_CHUNK
    btril = jnp.asarray((chunk_id[:, None] == chunk_id[None, :]) & (np.arange(ts)[:, None] >= np.arange(ts)[None, :]),
                        BF16)
    head_id = np.arange(BRANCH_W) // HEAD_DIM
    bones = jnp.asarray(head_id[:, None] == head_id[None, :], BF16)

    lb_cum = jnp.cumsum(jax.nn.softmax(hgrn_lb_logits.astype(F32), axis=0), axis=0)
    lower_bounds = lb_cum - lb_cum[0]

    fox_a = ((0, HEAD_DIM), (LANES, LANES + 8))
    fox_b = ((HEAD_DIM, LANES), (LANES + 8, LANES + 16))
    mla_a = ((0, LANES),)
    mla_b = ((LANES, 2 * LANES),)

    x2 = x.reshape(n, D_MODEL)
    for i in range(depth):
        x2 = _ffn(x2, norms[i, 0], ffn_a_wi[i], ffn_a_wo[i])
        w, bf, uqa, uqb, ukvk, ukvv = _pre_weights(w_in[i], b_fox_f[i], w_mla_uq[i], w_mla_ukv[i])
        (fq, fk, fv, mq, mk, mv, sq, sk, sv, hq, hk, hlf, hv, hg) = _pre(
            x2.reshape(b, s_len, D_MODEL), norms[i, 1], w, bf, uqa, uqb, ukvk, ukvv,
            mla_q_norm[i], mla_kv_norm[i], lower_bounds[i], (tril, selq, selk, rowq, rowk, rope))
        y_a = _softmax_attn(fq, fk, fv, 2 * LANES, fox_a, fox_b, "fox")
        y_b = _softmax_attn(mq, mk, mv, 2 * LANES, mla_a, mla_b, "mla")
        y_c = _sb_attn(sq, sk, sv, suffix)
        y_d = _hgrn(hq, hk, hlf, hv, hg, hgrn_out_norm[i], btril, bones)
        ys = [y.reshape(n, BRANCH_W) for y in (y_a, y_b, y_c, y_d)]
        x2 = _merge(x2, ys, norms[i, 1], w_in[i][:, IN_OFFS[14]:].astype(BF16),
                    w_branch[i].astype(BF16), w_out[i].astype(BF16))
        x2 = _ffn(x2, norms[i, 2], ffn_b_wi[i], ffn_b_wo[i])
        x2 = _ple(x2, p[i].reshape(n, PLE_DIM), norms[i, 3], w_ple_gate[i], w_ple_in[i],
                  final_norm if i == depth - 1 else None)
    return x2.reshape(b, s_len, D_MODEL)
```

```python
import functools

import jax
import jax.numpy as jnp
import numpy as np
from jax import lax
from jax.experimental import pallas as pl
from jax.experimental.pallas import tpu as pltpu

F32 = jnp.float32
BF16 = jnp.bfloat16

D_MODEL = 1024
PLE_DIM = 256
D_FF = 2816
EPS = 1e-6
LOG2E = 1.4426950408889634
N_BRANCH = 4
N_HEADS = 4
HEAD_DIM = 64
BRANCH_W = 256
MLA_NOPE = 64
MLA_ROPE = 32
MLA_Q_RANK = 256
MLA_KV_RANK = 128
ROPE_BASE = 10000.0
HGRN_CHUNK = 32
LANES = 128

IN_SPLITS = (256, 256, 256, 4, 256, 128, 32, 256, 256, 256, 256, 256, 256, 256, 4096)
IN_OFFS = tuple(int(o) for o in np.cumsum((0,) + IN_SPLITS))

PRE_COLS = dict(
    fq=(0, 256), fk=(256, 512), fv=(512, 768), ff_mckv=(768, 1024),
    mcq=(1024, 1280), mkr_pair=(1280, 1536),
    sq=(1536, 1792), sk=(1792, 2048), sv=(2048, 2304),
    hq=(2304, 2560), hf=(2560, 2816), hi=(2816, 3072), hg=(3072, 3328),
)
PRE_W = 3328

FF_CHUNK = 256
TOKEN_TILE = 512
ATTN_TILE = 1024
SB_KEY_TILE = 256
SB_SUBBLOCKS = 8
HGRN_TILE = 256
VMEM_LIMIT = 56 * 1024 * 1024


def _dot(a, b):
    return jnp.dot(a, b, preferred_element_type=F32)


def _dot_nt(a, b):
    return lax.dot_general(a, b, (((1,), (1,)), ((), ())), preferred_element_type=F32)


def _rms(x, g):
    return x * lax.rsqrt(jnp.mean(x * x, axis=-1, keepdims=True) + EPS) * g


def _log_sigmoid(x):
    return jnp.minimum(x, 0.0) - jnp.log1p(jnp.exp(-jnp.abs(x)))


def _split3(x):
    hi = x.astype(BF16)
    r1 = x - hi.astype(F32)
    mid = r1.astype(BF16)
    lo = (r1 - mid.astype(F32)).astype(BF16)
    return hi, mid, lo


def _params(*sem):
    return pltpu.CompilerParams(dimension_semantics=sem, vmem_limit_bytes=VMEM_LIMIT)


def _const_spec(shape):
    nd = len(shape)
    return pl.BlockSpec(shape, lambda *_: (0,) * nd, pipeline_mode=pl.Buffered(1))


def _ffn_kernel(x_ref, g_ref, wg_ref, wu_ref, wo_ref, o_ref, acc_ref):
    x = x_ref[...]
    h = _rms(x, g_ref[...]).astype(BF16)
    for c in range(D_FF // FF_CHUNK):
        sl = slice(c * FF_CHUNK, (c + 1) * FF_CHUNK)
        g = _dot(h, wg_ref[:, sl])
        u = _dot(h, wu_ref[:, sl])
        a = (g * jax.nn.sigmoid(g) * u).astype(BF16)
        part = _dot(a, wo_ref[sl, :])
        if c == 0:
            acc_ref[...] = part
        else:
            acc_ref[...] += part
    o_ref[...] = x + 0.5 * acc_ref[...]


def _ffn(x2, g, wi, wo):
    n = x2.shape[0]
    tm = min(TOKEN_TILE, n)
    wg = wi[:, :D_FF].astype(BF16)
    wu = wi[:, D_FF:].astype(BF16)
    return pl.pallas_call(
        _ffn_kernel,
        out_shape=jax.ShapeDtypeStruct((n, D_MODEL), F32),
        grid=(n // tm,),
        in_specs=[
            pl.BlockSpec((tm, D_MODEL), lambda i: (i, 0)),
            _const_spec((1, D_MODEL)),
            _const_spec((D_MODEL, D_FF)),
            _const_spec((D_MODEL, D_FF)),
            _const_spec((D_FF, D_MODEL)),
        ],
        out_specs=pl.BlockSpec((tm, D_MODEL), lambda i: (i, 0)),
        scratch_shapes=[pltpu.VMEM((tm, D_MODEL), F32)],
        compiler_params=_params("parallel"),
        name="ffn",
    )(x2, g.reshape(1, D_MODEL), wg, wu, wo.astype(BF16))


def _ple_kernel(x_ref, p_ref, g_ref, wpg_ref, wpe_ref, *rest, final):
    o_ref = rest[-1]
    x = x_ref[...]
    h = _rms(x, g_ref[...]).astype(BF16)
    gate = jax.nn.sigmoid(_dot(h, wpg_ref[...]))
    y = x + gate * _dot(p_ref[...].astype(BF16), wpe_ref[...])
    if final:
        y = _rms(y, rest[0][...])
    o_ref[...] = y


def _ple(x2, p2, g, wpg, wpe, final_g=None):
    n = x2.shape[0]
    tm = min(TOKEN_TILE, n)
    final = final_g is not None
    in_specs = [
        pl.BlockSpec((tm, D_MODEL), lambda i: (i, 0)),
        pl.BlockSpec((tm, PLE_DIM), lambda i: (i, 0)),
        _const_spec((1, D_MODEL)),
        _const_spec((D_MODEL, D_MODEL)),
        _const_spec((PLE_DIM, D_MODEL)),
    ]
    args = [x2, p2, g.reshape(1, D_MODEL), wpg.astype(BF16), wpe.astype(BF16)]
    if final:
        in_specs.append(_const_spec((1, D_MODEL)))
        args.append(final_g.reshape(1, D_MODEL))
    return pl.pallas_call(
        functools.partial(_ple_kernel, final=final),
        out_shape=jax.ShapeDtypeStruct((n, D_MODEL), F32),
        grid=(n // tm,),
        in_specs=in_specs,
        out_specs=pl.BlockSpec((tm, D_MODEL), lambda i: (i, 0)),
        compiler_params=_params("parallel"),
        name="ple_final" if final else "ple",
    )(*args)


def _pre_kernel(x_ref, g_ref, w_ref, bf_ref, tril_ref, selq_ref, selk_ref, rowq_ref, rowk_ref,
                qn_ref, kvn_ref, wuqa_ref, wuqb_ref, wukvk_ref, wukvv_ref, rope_ref, lb_ref,
                fq_ref, fk_ref, fv_ref, mq_ref, mk_ref, mv_ref, sq_ref, sk_ref, sv_ref,
                hq_ref, hk_ref, hlf_ref, hv_ref, hg_ref, carry_ref):
    @pl.when(pl.program_id(1) == 0)
    def _():
        carry_ref[...] = jnp.zeros_like(carry_ref)

    h = _rms(x_ref[0], g_ref[...]).astype(BF16)

    def proj(name):
        lo, hi = PRE_COLS[name]
        return _dot(h, w_ref[:, lo:hi])

    ff_mckv = proj("ff_mckv")
    log_f = _log_sigmoid(ff_mckv[:, :LANES] + bf_ref[...])
    tril = tril_ref[...]
    sums = _dot(tril, jnp.concatenate(_split3(log_f), axis=-1))
    c = carry_ref[...]
    for piece in range(3):
        c = c + sums[:, piece * LANES:(piece + 1) * LANES]
    carry_ref[...] = c[c.shape[0] - 1:, :]
    c_parts = jnp.concatenate(_split3(c * LOG2E), axis=-1)
    cq = (_dot(c_parts, selq_ref[...]) + rowq_ref[...]).astype(BF16)
    ck = (_dot(c_parts, selk_ref[...]) + rowk_ref[...]).astype(BF16)
    fq = (proj("fq") * LOG2E).astype(BF16)
    fk = proj("fk").astype(BF16)
    for p in range(2):
        pair = slice(p * LANES, (p + 1) * LANES)
        fq_ref[0, :, 2 * p * LANES:(2 * p + 1) * LANES] = fq[:, pair]
        fq_ref[0, :, (2 * p + 1) * LANES:(2 * p + 2) * LANES] = cq[:, pair]
        fk_ref[0, :, 2 * p * LANES:(2 * p + 1) * LANES] = fk[:, pair]
        fk_ref[0, :, (2 * p + 1) * LANES:(2 * p + 2) * LANES] = ck[:, pair]
    fv_ref[0] = proj("fv").astype(BF16)

    rope = rope_ref[...]
    cos_q, sin_q = rope[:, 0:LANES], rope[:, LANES:2 * LANES]
    cos_k, sin_k = rope[:, 2 * LANES:3 * LANES], rope[:, 3 * LANES:4 * LANES]
    cq_lat = _rms(proj("mcq"), qn_ref[...]).astype(BF16)
    qa = _dot(cq_lat, wuqa_ref[...])
    qb = _dot(cq_lat, wuqb_ref[...])
    ckv_lat = _rms(ff_mckv[:, LANES:], kvn_ref[...]).astype(BF16)
    k_nope = _dot(ckv_lat, wukvk_ref[...])
    mkr_pair = proj("mkr_pair")
    k_rope = mkr_pair[:, :LANES] * cos_k + mkr_pair[:, LANES:] * sin_k
    for hd in range(N_HEADS):
        sl = slice(hd * LANES, (hd + 1) * LANES)
        mq_ref[0, :, sl] = (qa[:, sl] * cos_q + qb[:, sl] * sin_q).astype(BF16)
        mk_ref[0, :, sl] = (k_nope[:, sl] + k_rope).astype(BF16)
    mv_ref[0] = _dot(ckv_lat, wukvv_ref[...]).astype(BF16)

    sq_ref[0] = (proj("sq") * LOG2E).astype(BF16)
    sk_ref[0] = proj("sk").astype(BF16)
    sv_ref[0] = proj("sv").astype(BF16)

    lb = lb_ref[...]
    f = lb + (1.0 - lb) * jax.nn.sigmoid(proj("hf"))
    hq_ref[0] = proj("hq").astype(BF16)
    hk_ref[0] = 1.0 - f
    hlf_ref[0] = jnp.log(f) * LOG2E
    hv_ref[0] = proj("hi").astype(BF16)
    hg = proj("hg")
    hg_ref[0] = (hg * jax.nn.sigmoid(hg)).astype(BF16)


def _pre_weights(w_in, b_f, w_uq, w_ukv):
    cols = lambda j: w_in[:, IN_OFFS[j]:IN_OFFS[j + 1]]
    zeros = lambda n: jnp.zeros((D_MODEL, n), F32)
    half = MLA_ROPE // 2
    mkr = cols(6)
    mkr_sw = jnp.concatenate([-mkr[:, half:], mkr[:, :half]], axis=1)
    pad_rope = lambda t: jnp.concatenate([zeros(MLA_NOPE), t, zeros(LANES - MLA_NOPE - MLA_ROPE)], axis=1)
    scale = HEAD_DIM ** -0.5
    w = jnp.concatenate([
        cols(0) * scale, cols(1), cols(2),
        jnp.concatenate([cols(3), zeros(LANES - N_HEADS)], axis=1),
        cols(5), cols(4), pad_rope(mkr), pad_rope(mkr_sw),
        cols(7) * scale, cols(8), cols(9),
        cols(10), cols(11), cols(12), cols(13)], axis=1).astype(BF16)
    bf = jnp.concatenate([b_f, jnp.zeros((LANES - N_HEADS,), F32)]).reshape(1, LANES)

    dq = MLA_NOPE + MLA_ROPE
    uq = w_uq.reshape(MLA_Q_RANK, N_HEADS, dq)
    zq = jnp.zeros((MLA_Q_RANK, N_HEADS, LANES - dq), F32)
    uqa = jnp.concatenate([uq, zq], axis=2).reshape(MLA_Q_RANK, N_HEADS * LANES)
    rope_cols = uq[:, :, MLA_NOPE:]
    rope_sw = jnp.concatenate([-rope_cols[:, :, half:], rope_cols[:, :, :half]], axis=2)
    uqb = jnp.concatenate([jnp.zeros((MLA_Q_RANK, N_HEADS, MLA_NOPE), F32), rope_sw, zq],
                          axis=2).reshape(MLA_Q_RANK, N_HEADS * LANES)
    ukv = w_ukv.reshape(MLA_KV_RANK, N_HEADS, MLA_NOPE + HEAD_DIM)
    ukvk = jnp.concatenate([ukv[:, :, :MLA_NOPE], jnp.zeros((MLA_KV_RANK, N_HEADS, LANES - MLA_NOPE), F32)],
                           axis=2).reshape(MLA_KV_RANK, N_HEADS * LANES)
    ukvv = ukv[:, :, MLA_NOPE:].reshape(MLA_KV_RANK, N_HEADS * HEAD_DIM)
    return w, bf, uqa.astype(BF16), uqb.astype(BF16), ukvk.astype(BF16), ukvv.astype(BF16)


def _fox_selectors():
    selq = np.zeros((3 * LANES, 2 * LANES), np.float32)
    selk = np.zeros((3 * LANES, 2 * LANES), np.float32)
    rowq = np.zeros((1, 2 * LANES), np.float32)
    rowk = np.zeros((1, 2 * LANES), np.float32)
    for hd in range(N_HEADS):
        base = (hd // 2) * LANES + (hd % 2) * 8
        for piece in range(3):
            selq[piece * LANES + hd, base + piece] = 1.0
            selk[piece * LANES + hd, base + 3 + piece] = -1.0
            rowq[0, base + 3 + piece] = 1.0
            rowk[0, base + piece] = 1.0
    return (jnp.asarray(selq, BF16), jnp.asarray(selk, BF16), jnp.asarray(rowq), jnp.asarray(rowk))


def _rope_tables(s_len):
    half = MLA_ROPE // 2
    inv = ROPE_BASE ** (-jnp.arange(half, dtype=F32) / half)
    ang = jnp.arange(s_len).astype(F32)[:, None] * inv[None, :]
    cos, sin = jnp.cos(ang), jnp.sin(ang)
    tail = jnp.zeros((s_len, LANES - MLA_NOPE - MLA_ROPE), F32)
    cos_t = jnp.concatenate([jnp.ones((s_len, MLA_NOPE), F32), cos, cos, tail], axis=1)
    sin_t = jnp.concatenate([jnp.zeros((s_len, MLA_NOPE), F32), sin, sin, tail], axis=1)
    scale = (MLA_NOPE + MLA_ROPE) ** -0.5 * LOG2E
    return jnp.concatenate([cos_t * scale, sin_t * scale, cos_t, sin_t], axis=1)


def _pre(x3, g, w, bf, uqa, uqb, ukvk, ukvv, qn, kvn, lb, consts):
    b, s_len, _ = x3.shape
    tm = min(TOKEN_TILE, s_len)
    tril, selq, selk, rowq, rowk, rope = consts
    tok = lambda width: pl.BlockSpec((1, tm, width), lambda bi, si: (bi, si, 0))
    bf16_out = lambda width: jax.ShapeDtypeStruct((b, s_len, width), BF16)
    f32_out = lambda width: jax.ShapeDtypeStruct((b, s_len, width), F32)
    in_specs = [
        tok(D_MODEL), _const_spec((1, D_MODEL)), _const_spec((D_MODEL, PRE_W)), _const_spec((1, LANES)),
        _const_spec((tm, tm)), _const_spec(selq.shape), _const_spec(selk.shape),
        _const_spec(rowq.shape), _const_spec(rowk.shape),
        _const_spec((1, MLA_Q_RANK)), _const_spec((1, MLA_KV_RANK)),
        _const_spec(uqa.shape), _const_spec(uqb.shape), _const_spec(ukvk.shape), _const_spec(ukvv.shape),
        pl.BlockSpec((tm, 4 * LANES), lambda bi, si: (si, 0)),
        _const_spec((1, BRANCH_W)),
    ]
    widths = [(512, BF16), (512, BF16), (256, BF16), (512, BF16), (512, BF16), (256, BF16),
              (256, BF16), (256, BF16), (256, BF16),
              (256, BF16), (256, F32), (256, F32), (256, BF16), (256, BF16)]
    return pl.pallas_call(
        _pre_kernel,
        out_shape=[bf16_out(wd) if dt == BF16 else f32_out(wd) for wd, dt in widths],
        grid=(b, s_len // tm),
        in_specs=in_specs,
        out_specs=[tok(wd) for wd, _ in widths],
        scratch_shapes=[pltpu.VMEM((1, LANES), F32)],
        compiler_params=_params("arbitrary", "arbitrary"),
        name="pre_mix",
    )(x3, g.reshape(1, D_MODEL), w, bf, tril, selq, selk, rowq, rowk,
      qn.reshape(1, -1), kvn.reshape(1, -1), uqa, uqb, ukvk, ukvv, rope, lb.reshape(1, -1))


def _lane_mask(shape, ranges):
    lane = lax.broadcasted_iota(jnp.int32, shape, 1)
    m = None
    for lo, hi in ranges:
        r = (lane >= lo) & (lane < hi)
        m = r if m is None else (m | r)
    return m


def _softmax_attn_kernel(q_ref, k_ref, v_ref, o_ref, qm_ref, s_ref, m_ref, l_ref, a_ref, acc_ref, *, t, lanes_a,
                         lanes_b):
    qi = pl.program_id(2)
    q = q_ref[0]
    zero = jnp.zeros_like(q)
    qm_ref[0] = jnp.where(_lane_mask(q.shape, lanes_a), q, zero)
    qm_ref[1] = jnp.where(_lane_mask(q.shape, lanes_b), q, zero)
    m_ref[...] = jnp.full(m_ref.shape, -1e30, F32)
    l_ref[...] = jnp.zeros(l_ref.shape, F32)
    acc_ref[...] = jnp.zeros(acc_ref.shape, F32)

    half = t // 2
    top, bottom = slice(0, half), slice(half, t)

    def logits(hd, rows, key0, nkeys, cols, triangle):
        s = _dot_nt(qm_ref[hd, rows, :], k_ref[0, pl.ds(pl.multiple_of(key0, half), nkeys), :])
        if triangle:
            row = lax.broadcasted_iota(jnp.int32, s.shape, 0)
            col = lax.broadcasted_iota(jnp.int32, s.shape, 1)
            s = jnp.where(col <= row, s, -1e30)
        s_ref[hd, rows, cols] = s
        return jnp.max(s, axis=-1, keepdims=True)

    def fold_max(hd, rows, row_max):
        m = m_ref[hd, rows, :]
        m_new = jnp.maximum(m, row_max)
        a_ref[hd, rows, :] = jnp.exp2(m - m_new)
        m_ref[hd, rows, :] = m_new

    def weigh(hd, rows, cols, key0, nkeys):
        width = cols.stop - cols.start
        p = jnp.exp2(s_ref[hd, rows, cols] - jnp.concatenate([m_ref[hd, rows, :]] * (width // LANES), axis=1))
        alpha = a_ref[hd, rows, :]
        l_ref[hd, rows, :] = alpha * l_ref[hd, rows, :] + jnp.sum(p, axis=-1, keepdims=True)
        vb = v_ref[0, pl.ds(pl.multiple_of(key0, half), nkeys), :]
        acc_ref[hd, rows, :] = alpha * acc_ref[hd, rows, :] + _dot(p.astype(BF16), vb)

    def start(hd, j, diagonal):
        if not diagonal:
            fold_max(hd, slice(0, t), logits(hd, slice(0, t), j * t, t, slice(0, t), False))
            return
        fold_max(hd, top, logits(hd, top, j * t, half, top, True))
        fold_max(hd, bottom, jnp.maximum(logits(hd, bottom, j * t, half, top, False),
                                         logits(hd, bottom, j * t + half, half, bottom, True)))

    def finish(hd, j, diagonal=False):
        if not diagonal:
            weigh(hd, slice(0, t), slice(0, t), j * t, t)
            return
        weigh(hd, top, top, j * t, half)
        weigh(hd, bottom, slice(0, t), j * t, t)

    def body(j, carry):
        finish(0, j)
        start(1, j, False)
        finish(1, j)
        start(0, j + 1, False)
        return carry

    @pl.when(qi == 0)
    def _():
        start(0, 0, True)

    @pl.when(qi > 0)
    def _():
        start(0, 0, False)
        lax.fori_loop(0, qi - 1, body, 0)
        last = qi - 1
        finish(0, last)
        start(1, last, False)
        finish(1, last)
        start(0, qi, True)

    finish(0, qi, True)
    start(1, qi, True)
    finish(1, qi, True)
    first_half = _lane_mask((t, LANES), ((0, HEAD_DIM),))
    o_ref[0] = jnp.where(first_half, acc_ref[0] / l_ref[0], acc_ref[1] / l_ref[1]).astype(BF16)


def _softmax_attn(q, k, v, kw, lanes_a, lanes_b, name):
    b, s_len, _ = q.shape
    t = min(ATTN_TILE, s_len)
    stat = pltpu.VMEM((2, t, LANES), F32)
    return pl.pallas_call(
        functools.partial(_softmax_attn_kernel, t=t, lanes_a=lanes_a, lanes_b=lanes_b),
        out_shape=jax.ShapeDtypeStruct((b, s_len, BRANCH_W), BF16),
        grid=(b, 2, s_len // t),
        in_specs=[
            pl.BlockSpec((1, t, kw), lambda bi, hp, qi: (bi, qi, hp)),
            pl.BlockSpec((1, s_len, kw), lambda bi, hp, qi: (bi, 0, hp)),
            pl.BlockSpec((1, s_len, LANES), lambda bi, hp, qi: (bi, 0, hp)),
        ],
        out_specs=pl.BlockSpec((1, t, LANES), lambda bi, hp, qi: (bi, qi, hp)),
        scratch_shapes=[pltpu.VMEM((2, t, kw), BF16), pltpu.VMEM((2, t, t), F32), stat, stat, stat, stat],
        compiler_params=_params("parallel", "parallel", "arbitrary"),
        name=name,
    )(q, k, v)


def _sb_attn_kernel(q_ref, k_ref, v_ref, u_ref, o_ref, qm_ref, rest_ref, acc_ref, *, tk, nsub):
    tq = nsub * tk
    qi = pl.program_id(2)
    q = q_ref[0]
    zero = jnp.zeros_like(q)
    qm_ref[0] = jnp.where(_lane_mask(q.shape, ((0, HEAD_DIM),)), q, zero)
    qm_ref[1] = jnp.where(_lane_mask(q.shape, ((HEAD_DIM, 2 * HEAD_DIM),)), q, zero)
    rest_ref[...] = jnp.zeros(rest_ref.shape, F32)
    acc_ref[...] = jnp.zeros(acc_ref.shape, F32)

    def weights(hd, r0, nrows, blk, diag):
        rows = pl.ds(r0, nrows)
        off = pl.multiple_of(blk * tk, tk)
        z = _dot_nt(qm_ref[hd, rows, :], k_ref[0, pl.ds(off, tk), :])
        neg_abs = pltpu.bitcast(pltpu.bitcast(z, jnp.uint32) | jnp.uint32(0x80000000), F32)
        log_beta = jnp.minimum(z, 0.0) - jnp.log2(1.0 + jnp.exp2(neg_abs))
        log_rest = log_beta - z
        if diag:
            row = lax.broadcasted_iota(jnp.int32, (tk, tk), 0)
            col = lax.broadcasted_iota(jnp.int32, (tk, tk), 1)
            strict = col < row
            log_rest = jnp.where(strict, log_rest, 0.0)
        rest = rest_ref[hd, rows, :]
        later = _dot(log_rest.astype(BF16), u_ref[...]) + jnp.concatenate([rest] * (tk // LANES), axis=1)
        a = jnp.exp2(log_beta + later)
        if diag:
            a = jnp.where(strict, a, 0.0)
        rest_ref[hd, rows, :] = rest + jnp.sum(log_rest, axis=-1, keepdims=True)
        return a.astype(BF16)

    def block(hd, r0, nrows, blk, diag):
        a = weights(hd, r0, nrows, blk, diag)
        acc_ref[hd, pl.ds(r0, nrows), :] += _dot(a, v_ref[0, pl.ds(pl.multiple_of(blk * tk, tk), tk), :])

    for hd in range(2):
        for d in reversed(range(nsub)):
            block(hd, d * tk, tk, nsub * qi + d, True)
            if d < nsub - 1:
                block(hd, (d + 1) * tk, tq - (d + 1) * tk, nsub * qi + d, False)

    def body(j, carry):
        chunk = qi - 1 - j
        off = pl.multiple_of(chunk * tq, tq)
        for hd in range(2):
            parts = [None] * nsub
            for d in reversed(range(nsub)):
                parts[d] = weights(hd, 0, tq, nsub * chunk + d, False)
            acc_ref[hd] += _dot(jnp.concatenate(parts, axis=1), v_ref[0, pl.ds(off, tq), :])
        return carry

    lax.fori_loop(0, qi, body, 0)
    first_half = _lane_mask((tq, LANES), ((0, HEAD_DIM),))
    o_ref[0] = jnp.where(first_half, acc_ref[0], acc_ref[1]).astype(BF16)


def _sb_attn(q, k, v, u):
    b, s_len, _ = q.shape
    tk = u.shape[0]
    nsub = min(SB_SUBBLOCKS, s_len // tk)
    tq = nsub * tk
    return pl.pallas_call(
        functools.partial(_sb_attn_kernel, tk=tk, nsub=nsub),
        out_shape=jax.ShapeDtypeStruct((b, s_len, BRANCH_W), BF16),
        grid=(b, 2, s_len // tq),
        in_specs=[
            pl.BlockSpec((1, tq, LANES), lambda bi, hp, qi: (bi, qi, hp)),
            pl.BlockSpec((1, s_len, LANES), lambda bi, hp, qi: (bi, 0, hp)),
            pl.BlockSpec((1, s_len, LANES), lambda bi, hp, qi: (bi, 0, hp)),
            _const_spec((tk, tk)),
        ],
        out_specs=pl.BlockSpec((1, tq, LANES), lambda bi, hp, qi: (bi, qi, hp)),
        scratch_shapes=[pltpu.VMEM((2, tq, LANES), BF16), pltpu.VMEM((2, tq, LANES), F32),
                        pltpu.VMEM((2, tq, LANES), F32)],
        compiler_params=_params("parallel", "parallel", "arbitrary"),
        name="stick_breaking",
    )(q, k, v, u)


def _hgrn_kernel(q_ref, k_ref, lf_ref, v_ref, gate_ref, on_ref, btril_ref, bones_ref, o_ref, st_ref, *, ts):
    @pl.when(pl.program_id(1) == 0)
    def _():
        st_ref[...] = jnp.zeros_like(st_ref)

    c = HGRN_CHUNK
    w = BRANCH_W
    btril = btril_ref[...]
    bones = bones_ref[...]
    head_mask = bones.astype(F32)
    cum = None
    for piece in _split3(lf_ref[0]):
        part = _dot(btril, piece)
        cum = part if cum is None else cum + part

    sub = 8
    groups = [(g * sub, (g + 1) * sub) for g in range(c // sub)]
    causal = [(t0 + lax.broadcasted_iota(jnp.int32, (sub, ns, w), 0)) >= lax.broadcasted_iota(jnp.int32, (sub, ns, w), 1)
              for t0, ns in groups]
    outs = []
    for ci in range(ts // c):
        r = slice(ci * c, (ci + 1) * c)
        bc = cum[r]
        qc = q_ref[0, r, :].astype(F32)
        kc = k_ref[0, r, :]
        vc = v_ref[0, r, :].astype(F32)
        prods = []
        for (t0, ns), mask in zip(groups, causal):
            bt = bc[t0:t0 + sub]
            dec = jnp.exp2(jnp.where(mask, bt[:, None, :] - bc[None, :ns, :], -jnp.inf))
            prods.append((qc[t0:t0 + sub, None, :] * dec * kc[None, :ns, :]).reshape(sub * ns, w).astype(BF16))
        scores = _dot(jnp.concatenate(prods, axis=0), bones)
        o_parts = []
        row0 = 0
        for t0, ns in groups:
            sc = scores[row0:row0 + sub * ns].reshape(sub, ns, w)
            o_parts.append(jnp.sum(sc * vc[None, :ns, :], axis=1))
            row0 += sub * ns
        o_intra = jnp.concatenate(o_parts, axis=0)
        st = st_ref[...]
        o_inter = _dot_nt((qc * jnp.exp2(bc)).astype(BF16), st.astype(BF16))
        b_last = bc[c - 1:c, :]
        k_dec = (kc * jnp.exp2(b_last - bc)).astype(BF16)
        upd = _dot(vc.T.astype(BF16), k_dec)
        st_ref[...] = st * jnp.exp2(b_last) + upd * head_mask
        outs.append(o_intra + o_inter)
    o = jnp.concatenate(outs, axis=0)
    sq_hi = (o * o).astype(BF16)
    sq_lo = (o * o - sq_hi.astype(F32)).astype(BF16)
    ms = (_dot(sq_hi, bones) + _dot(sq_lo, bones)) * (1.0 / HEAD_DIM)
    y = o * lax.rsqrt(ms + EPS) * on_ref[...] * gate_ref[0].astype(F32)
    o_ref[0] = y.astype(BF16)


def _hgrn(q, k, lf, v, gate, out_norm, btril, bones):
    b, s_len, w = q.shape
    ts = btril.shape[0]
    tok = pl.BlockSpec((1, ts, w), lambda bi, si: (bi, si, 0))
    return pl.pallas_call(
        functools.partial(_hgrn_kernel, ts=ts),
        out_shape=jax.ShapeDtypeStruct((b, s_len, w), BF16),
        grid=(b, s_len // ts),
        in_specs=[tok, tok, tok, tok, tok, _const_spec((1, w)), _const_spec((ts, ts)), _const_spec((w, w))],
        out_specs=tok,
        scratch_shapes=[pltpu.VMEM((w, w), F32)],
        compiler_params=_params("arbitrary", "arbitrary"),
        name="hgrn2",
    )(q, k, lf, v, gate, out_norm.reshape(1, w), btril, bones)


def _merge_kernel(x_ref, ya_ref, yb_ref, yc_ref, yd_ref, g_ref, wgate_ref, wbr_ref, wout_ref, o_ref):
    x = x_ref[...]
    h = _rms(x, g_ref[...]).astype(BF16)
    mixed = None
    for m, y_ref in enumerate((ya_ref, yb_ref, yc_ref, yd_ref)):
        gate = jax.nn.sigmoid(_dot(h, wgate_ref[:, m * D_MODEL:(m + 1) * D_MODEL]))
        term = gate * _dot(y_ref[...], wbr_ref[m])
        mixed = term if mixed is None else mixed + term
    o_ref[...] = x + _dot(mixed.astype(BF16), wout_ref[...])


def _merge(x2, ys, g, wgate, wbr, wout):
    n = x2.shape[0]
    tm = min(TOKEN_TILE, n)
    ytok = pl.BlockSpec((tm, BRANCH_W), lambda i: (i, 0))
    return pl.pallas_call(
        _merge_kernel,
        out_shape=jax.ShapeDtypeStruct((n, D_MODEL), F32),
        grid=(n // tm,),
        in_specs=[pl.BlockSpec((tm, D_MODEL), lambda i: (i, 0)), ytok, ytok, ytok, ytok,
                  _const_spec((1, D_MODEL)), _const_spec((D_MODEL, N_BRANCH * D_MODEL)),
                  _const_spec((N_BRANCH, BRANCH_W, D_MODEL)), _const_spec((D_MODEL, D_MODEL))],
        out_specs=pl.BlockSpec((tm, D_MODEL), lambda i: (i, 0)),
        compiler_params=_params("parallel"),
        name="merge",
    )(x2, *ys, g.reshape(1, D_MODEL), wgate, wbr, wout)


def kernel(x, p, w_in, b_fox_f, mla_q_norm, w_mla_uq, mla_kv_norm, w_mla_ukv, hgrn_lb_logits, hgrn_out_norm,
           w_branch, w_out, ffn_a_wi, ffn_a_wo, ffn_b_wi, ffn_b_wo, w_ple_in, w_ple_gate, norms, final_norm):
    b, s_len, _ = x.shape
    n = b * s_len
    depth = w_in.shape[0]
    tm = min(TOKEN_TILE, s_len)
    t = min(SB_KEY_TILE, s_len)
    ts = min(HGRN_TILE, s_len)

    tril = jnp.tril(jnp.ones((tm, tm), F32)).astype(BF16)
    selq, selk, rowq, rowk = _fox_selectors()
    rope = _rope_tables(s_len)
    suffix = jnp.tril(jnp.ones((t, t), F32), -1).astype(BF16)
    chunk_id = np.arange(ts) // HGRN_CHUNK
    btril = jnp.asarray((chunk_id[:, None] == chunk_id[None, :]) & (np.arange(ts)[:, None] >= np.arange(ts)[None, :]),
                        BF16)
    head_id = np.arange(BRANCH_W) // HEAD_DIM
    bones = jnp.asarray(head_id[:, None] == head_id[None, :], BF16)

    lb_cum = jnp.cumsum(jax.nn.softmax(hgrn_lb_logits.astype(F32), axis=0), axis=0)
    lower_bounds = lb_cum - lb_cum[0]

    fox_a = ((0, HEAD_DIM), (LANES, LANES + 8))
    fox_b = ((HEAD_DIM, LANES), (LANES + 8, LANES + 16))
    mla_a = ((0, LANES),)
    mla_b = ((LANES, 2 * LANES),)

    x2 = x.reshape(n, D_MODEL)
    for i in range(depth):
        x2 = _ffn(x2, norms[i, 0], ffn_a_wi[i], ffn_a_wo[i])
        w, bf, uqa, uqb, ukvk, ukvv = _pre_weights(w_in[i], b_fox_f[i], w_mla_uq[i], w_mla_ukv[i])
        (fq, fk, fv, mq, mk, mv, sq, sk, sv, hq, hk, hlf, hv, hg) = _pre(
            x2.reshape(b, s_len, D_MODEL), norms[i, 1], w, bf, uqa, uqb, ukvk, ukvv,
            mla_q_norm[i], mla_kv_norm[i], lower_bounds[i], (tril, selq, selk, rowq, rowk, rope))
        y_a = _softmax_attn(fq, fk, fv, 2 * LANES, fox_a, fox_b, "fox")
        y_b = _softmax_attn(mq, mk, mv, 2 * LANES, mla_a, mla_b, "mla")
        y_c = _sb_attn(sq, sk, sv, suffix)
        y_d = _hgrn(hq, hk, hlf, hv, hg, hgrn_out_norm[i], btril, bones)
        ys = [y.reshape(n, BRANCH_W) for y in (y_a, y_b, y_c, y_d)]
        x2 = _merge(x2, ys, norms[i, 1], w_in[i][:, IN_OFFS[14]:].astype(BF16),
                    w_branch[i].astype(BF16), w_out[i].astype(BF16))
        x2 = _ffn(x2, norms[i, 2], ffn_b_wi[i], ffn_b_wo[i])
        x2 = _ple(x2, p[i].reshape(n, PLE_DIM), norms[i, 3], w_ple_gate[i], w_ple_in[i],
                  final_norm if i == depth - 1 else None)
    return x2.reshape(b, s_len, D_MODEL)
```
